```python
import math
import jax, jax.numpy as jnp
from jax import lax
import numpy as np

D_MODEL = 4096
BATCH = 4
SEQ = 4096
DEPTH = 2

SSD_EXPAND = 2
SSD_D_INNER = SSD_EXPAND * D_MODEL
SSD_HEAD_DIM = 64
SSD_HEADS = SSD_D_INNER // SSD_HEAD_DIM
SSD_GROUPS = 8
SSD_STATE = 128
SSD_CONV = 4
SSD_CHUNK = 128
SSD_CONV_CH = SSD_D_INNER + 2 * SSD_GROUPS * SSD_STATE
ATT_HEADS = 32
ATT_HEAD_DIM = 128
ATT_KV_GROUPS = 4
ATT_WIDTH = ATT_HEADS * ATT_HEAD_DIM
ATT_KV_WIDTH = ATT_KV_GROUPS * ATT_HEAD_DIM
IDX_HEADS = 32
IDX_DIM = 128
TOPK_MAX = 256
Q_BLOCK = 128
D_FF = 2 * D_MODEL
FFN_CONV = 3
N_MOD = 6
EPS = 1e-6

IN_SPLIT_SIZES = (SSD_D_INNER, SSD_CONV_CH, SSD_HEADS, ATT_WIDTH, ATT_KV_WIDTH, ATT_KV_WIDTH,
                  IDX_HEADS * IDX_DIM, IDX_DIM, IDX_HEADS, D_MODEL, D_MODEL)
IN_SPLIT_POINTS = tuple(int(v) for v in np.cumsum(IN_SPLIT_SIZES)[:-1])
N_IN = int(sum(IN_SPLIT_SIZES))

kernel_name = 'hybrid_ssd_dsa_convffn_block'


def rms_norm(x, g):
    xf = x.astype(jnp.float32)
    y = xf * lax.rsqrt(jnp.mean(xf * xf, axis=-1, keepdims=True) + EPS)
    return (y * g.astype(jnp.float32)).astype(x.dtype)


def causal_dwconv(x, w, b):
    k = w.shape[0]
    y = lax.conv_general_dilated(
        x, w[:, None, :].astype(x.dtype), window_strides=(1,), padding=[(k - 1, 0)],
        dimension_numbers=('NWC', 'WIO', 'NWC'), feature_group_count=x.shape[-1])
    return y + b.astype(x.dtype)


def ssd_chunked(x, dt, a, bm, cm):
    bsz, s, h, p = x.shape
    g, n = bm.shape[2], bm.shape[3]
    hg = h // g
    nc = s // SSD_CHUNK
    q = SSD_CHUNK
    xdt = jnp.moveaxis((x * dt[..., None]).reshape(bsz, nc, q, g, hg, p), 1, 0)
    da = jnp.moveaxis((dt * a).reshape(bsz, nc, q, g, hg), 1, 0)
    bc = jnp.moveaxis(bm.reshape(bsz, nc, q, g, n), 1, 0)
    cc = jnp.moveaxis(cm.reshape(bsz, nc, q, g, n), 1, 0)
    causal = jnp.tril(jnp.ones((q, q), dtype=bool))[None, :, :, None, None]

    def step(state, inp):
        xq, daq, bq, cq = inp
        acum = jnp.cumsum(daq, axis=1)
        seg = acum[:, :, None] - acum[:, None, :]
        decay = jnp.exp(jnp.where(causal, seg, -jnp.inf))
        cb = jnp.einsum('bign,bjgn->bijg', cq, bq)
        y_diag = jnp.einsum('bijg,bijgh,bjghp->bighp', cb, decay, xq)
        y_off = jnp.einsum('bign,bghpn,bigh->bighp', cq, state, jnp.exp(acum))
        last = acum[:, -1]
        w_state = jnp.exp(last[:, None] - acum)
        new_state = state * jnp.exp(last)[..., None, None] + jnp.einsum(
            'bjgn,bjgh,bjghp->bghpn', bq, w_state, xq)
        return new_state, y_diag + y_off

    state0 = jnp.zeros((bsz, g, hg, p, n), jnp.float32)
    _, y = lax.scan(step, state0, (xdt, da, bc, cc))
    return jnp.moveaxis(y, 0, 1).reshape(bsz, s, h, p)


def ssd_branch(z, xbc, dt_raw, conv_w, conv_b, dt_bias, a_log, d_skip, norm_g):
    bsz, s, _ = z.shape
    f32 = jnp.float32
    xbc = jax.nn.silu(causal_dwconv(xbc, conv_w, conv_b)).astype(f32)
    gn = SSD_GROUPS * SSD_STATE
    xs, bm, cm = jnp.split(xbc, [SSD_D_INNER, SSD_D_INNER + gn], axis=-1)
    dt = jax.nn.softplus(dt_raw.astype(f32) + dt_bias.astype(f32))
    a = -jnp.exp(a_log.astype(f32))
    xh = xs.reshape(bsz, s, SSD_HEADS, SSD_HEAD_DIM)
    y = ssd_chunked(xh, dt, a,
                    bm.reshape(bsz, s, SSD_GROUPS, SSD_STATE),
                    cm.reshape(bsz, s, SSD_GROUPS, SSD_STATE))
    y = y + d_skip.astype(f32)[:, None] * xh
    yg = (y.reshape(bsz, s, SSD_D_INNER) * jax.nn.silu(z.astype(f32))).reshape(
        bsz, s, SSD_GROUPS, SSD_D_INNER // SSD_GROUPS)
    yg = yg * lax.rsqrt(jnp.mean(yg * yg, axis=-1, keepdims=True) + EPS)
    y = yg.reshape(bsz, s, SSD_D_INNER) * norm_g.astype(f32)
    return y.astype(z.dtype)


def dsa_branch(q, k, v, qi, ki, wi, q_norm_g, k_norm_g, ki_norm_g):
    bsz, s, _ = q.shape
    f32 = jnp.float32
    hpg = ATT_HEADS // ATT_KV_GROUPS
    q = rms_norm(q.reshape(bsz, s, ATT_HEADS, ATT_HEAD_DIM), q_norm_g)
    k = rms_norm(k.reshape(bsz, s, ATT_KV_GROUPS, ATT_HEAD_DIM), k_norm_g)
    v = v.reshape(bsz, s, ATT_KV_GROUPS, ATT_HEAD_DIM)
    qi = qi.reshape(bsz, s, IDX_HEADS, IDX_DIM)
    ki = rms_norm(ki, ki_norm_g)
    wi = wi.astype(f32) * (IDX_HEADS ** -0.5)
    k_sel = min(TOPK_MAX, s // 4)
    nb = s // Q_BLOCK
    key_pos = jnp.arange(s)
    bidx = jnp.arange(bsz)[:, None]

    def to_blocks(t):
        return t.reshape((bsz, nb, Q_BLOCK) + t.shape[2:]).swapaxes(0, 1)

    def block(inp):
        qb, qib, wib, t0 = inp
        tpos = t0 + jnp.arange(Q_BLOCK)
        logits = jnp.einsum('bthd,bsd->bths', qib, ki,
                            preferred_element_type=f32) * (IDX_DIM ** -0.5)
        score = jnp.einsum('bth,bths->bts', wib, jax.nn.relu(logits))
        causal = key_pos[None, :] <= tpos[:, None]
        score = jnp.where(causal[None], score, -jnp.inf)
        _, idx = lax.top_k(score, k_sel)
        valid = idx <= tpos[None, :, None]
        flat = idx.reshape(bsz, Q_BLOCK * k_sel)
        kg = k[bidx, flat].reshape(bsz, Q_BLOCK, k_sel, ATT_KV_GROUPS, ATT_HEAD_DIM)
        vg = v[bidx, flat].reshape(bsz, Q_BLOCK, k_sel, ATT_KV_GROUPS, ATT_HEAD_DIM)
        qg = qb.reshape(bsz, Q_BLOCK, ATT_KV_GROUPS, hpg, ATT_HEAD_DIM)
        att = jnp.einsum('btghd,btkgd->btghk', qg, kg,
                         preferred_element_type=f32) * (ATT_HEAD_DIM ** -0.5)
        att = jnp.where(valid[:, :, None, None, :], att, -jnp.inf)
        prob = jax.nn.softmax(att, axis=-1).astype(vg.dtype)
        o = jnp.einsum('btghk,btkgd->btghd', prob, vg)
        return o.reshape(bsz, Q_BLOCK, ATT_WIDTH)

    starts = jnp.arange(nb) * Q_BLOCK
    out = lax.map(block, (to_blocks(q), to_blocks(qi), to_blocks(wi), starts))
    return out.swapaxes(0, 1).reshape(bsz, s, ATT_WIDTH)


def setup_inputs(seed: int = 0) -> dict:
    key = jax.random.key(seed)
    ks = jax.random.split(key, 24)
    f32 = jnp.float32

    def nrm(k, shape, scale):
        return jax.random.normal(k, shape, f32) * scale

    dt0 = jnp.exp(jax.random.uniform(ks[9], (DEPTH, SSD_HEADS), f32,
                                     math.log(1e-3), math.log(1e-1)))
    return {
        'x': nrm(ks[0], (BATCH, SEQ, D_MODEL), 1.0),
        'c': nrm(ks[1], (BATCH, D_MODEL), 1.0),
        'w_ada': nrm(ks[2], (D_MODEL, N_MOD * D_MODEL), 0.5 * D_MODEL ** -0.5),
        'b_ada': nrm(ks[3], (N_MOD * D_MODEL,), 0.02),
        'ada_table': nrm(ks[4], (DEPTH, N_MOD, D_MODEL), 0.02),
        'norm1_g': 1.0 + nrm(ks[5], (DEPTH, D_MODEL), 0.02),
        'w_in': nrm(ks[6], (DEPTH, D_MODEL, N_IN), D_MODEL ** -0.5),
        'ssd_conv_w': nrm(ks[7], (DEPTH, SSD_CONV, SSD_CONV_CH), SSD_CONV ** -0.5),
        'ssd_conv_b': nrm(ks[8], (DEPTH, SSD_CONV_CH), 0.02),
        'ssd_dt_bias': dt0 + jnp.log(-jnp.expm1(-dt0)),
        'ssd_a_log': jnp.log(jax.random.uniform(ks[10], (DEPTH, SSD_HEADS), f32, 1.0, 16.0)),
        'ssd_d': 1.0 + nrm(ks[11], (DEPTH, SSD_HEADS), 0.02),
        'ssd_norm_g': 1.0 + nrm(ks[12], (DEPTH, SSD_D_INNER), 0.02),
        'w_ssd_out': nrm(ks[13], (DEPTH, SSD_D_INNER, D_MODEL), SSD_D_INNER ** -0.5),
        'q_norm_g': 1.0 + nrm(ks[14], (DEPTH, ATT_HEAD_DIM), 0.02),
        'k_norm_g': 1.0 + nrm(ks[15], (DEPTH, ATT_HEAD_DIM), 0.02),
        'idx_k_norm_g': 1.0 + nrm(ks[16], (DEPTH, IDX_DIM), 0.02),
        'w_att_out': nrm(ks[17], (DEPTH, ATT_WIDTH, D_MODEL), ATT_WIDTH ** -0.5),
        'w_o': nrm(ks[18], (DEPTH, D_MODEL, D_MODEL), D_MODEL ** -0.5),
        'norm2_g': 1.0 + nrm(ks[19], (DEPTH, D_MODEL), 0.02),
        'w_up': nrm(ks[20], (DEPTH, D_MODEL, 2 * D_FF), D_MODEL ** -0.5),
        'ffn_conv_w': nrm(ks[21], (DEPTH, FFN_CONV, 2 * D_FF), FFN_CONV ** -0.5),
        'ffn_conv_b': nrm(ks[22], (DEPTH, 2 * D_FF), 0.02),
        'w_down': nrm(ks[23], (DEPTH, D_FF, D_MODEL), D_FF ** -0.5),
    }


def reference(x, c, w_ada, b_ada, ada_table, norm1_g, w_in, ssd_conv_w, ssd_conv_b,
              ssd_dt_bias, ssd_a_log, ssd_d, ssd_norm_g, w_ssd_out, q_norm_g, k_norm_g,
              idx_k_norm_g, w_att_out, w_o, norm2_g, w_up, ffn_conv_w, ffn_conv_b, w_down):
    bsz = x.shape[0]
    mod_shared = (jax.nn.silu(c) @ w_ada + b_ada).reshape(bsz, N_MOD, D_MODEL)
    for l in range(DEPTH):
        mod = mod_shared + ada_table[l][None]
        shift1, scale1, gate1, shift2, scale2, gate2 = [mod[:, i, None, :] for i in range(N_MOD)]

        h = rms_norm(x, norm1_g[l]) * (1.0 + scale1) + shift1
        proj = h @ w_in[l]
        z, xbc, dt_raw, q, k, v, qi, ki, wi, g_ssd, g_att = jnp.split(
            proj, IN_SPLIT_POINTS, axis=-1)
        y_ssd = ssd_branch(z, xbc, dt_raw, ssd_conv_w[l], ssd_conv_b[l], ssd_dt_bias[l],
                           ssd_a_log[l], ssd_d[l], ssd_norm_g[l]) @ w_ssd_out[l]
        y_att = dsa_branch(q, k, v, qi, ki, wi, q_norm_g[l], k_norm_g[l],
                           idx_k_norm_g[l]) @ w_att_out[l]
        merged = jax.nn.sigmoid(g_ssd) * y_ssd + jax.nn.sigmoid(g_att) * y_att
        x = x + gate1 * (merged @ w_o[l])

        h = rms_norm(x, norm2_g[l]) * (1.0 + scale2) + shift2
        up = causal_dwconv(h @ w_up[l], ffn_conv_w[l], ffn_conv_b[l])
        a, b = jnp.split(up, 2, axis=-1)
        x = x + gate2 * ((jax.nn.silu(a) * b) @ w_down[l])
    return x
```

```python
import functools
import math
from typing import NamedTuple

import jax
import jax.numpy as jnp
from jax import lax
from jax.experimental import pallas as pl
from jax.experimental.pallas import tpu as pltpu

V7X_VMEM_BYTES = 64 * 1024 * 1024
LANE = 128
SUBLANE = 8
VMEM_BUDGET = V7X_VMEM_BYTES - 8 * 1024 * 1024
COMPILER_SCRATCH = 4 * 1024 * 1024

F32 = jnp.float32
BF16 = jnp.bfloat16
NEG_BIG = -1e30
INT_MIN = -(2 ** 31)


class Cfg(NamedTuple):
    ssd_head_dim: int = 64
    ssd_groups: int = 8
    ssd_state: int = 128
    ssd_conv: int = 4
    ssd_chunk: int = 128
    ssd_expand: int = 2
    att_heads: int = 32
    att_head_dim: int = 128
    att_kv_groups: int = 4
    idx_heads: int = 32
    idx_dim: int = 128
    topk_max: int = 256
    q_block: int = 128
    ffn_mult: int = 2
    ffn_conv: int = 3
    n_mod: int = 6
    eps: float = 1e-6


def _round_up(x, m):
    return (x + m - 1) // m * m


def _params(sem, vmem_bytes):
    return pltpu.CompilerParams(dimension_semantics=sem,
                                vmem_limit_bytes=int(min(vmem_bytes + COMPILER_SCRATCH, VMEM_BUDGET)))


def _silu(x):
    return x * (1.0 / (1.0 + jnp.exp(-x)))


def _sigmoid(x):
    return 1.0 / (1.0 + jnp.exp(-x))


def _rms(x, eps):
    return x * lax.rsqrt(jnp.mean(x * x, axis=-1, keepdims=True) + eps)


def _mm_body(*refs, nk, n_extra, epilogue):
    a_ref, b_ref = refs[0], refs[1]
    extra = refs[2:2 + n_extra]
    o_ref = refs[2 + n_extra]
    part = jnp.dot(a_ref[...], b_ref[...], preferred_element_type=F32)
    if nk == 1:
        o_ref[...] = epilogue(part, *[e[...] for e in extra]).astype(o_ref.dtype)
        return
    acc_ref = refs[3 + n_extra]
    k = pl.program_id(2)

    @pl.when(k == 0)
    def _():
        acc_ref[...] = part

    @pl.when(k > 0)
    def _():
        acc_ref[...] += part

    @pl.when(k == nk - 1)
    def _():
        o_ref[...] = epilogue(acc_ref[...], *[e[...] for e in extra]).astype(o_ref.dtype)


def _mm_blocks(m, k, n, a_bytes, b_bytes, out_bytes, n_tile_extra, row_group):
    for tm, tn, tk in ((1024, 1024, 4096), (1024, 1024, 2048), (1024, 512, 4096),
                       (512, 512, 4096), (512, 512, 2048), (1024, 128, 4096), (512, 256, 2048),
                       (256, 256, 2048), (256, 128, 1024), (128, 128, 512)):
        tk = min(tk, k)
        if m % tm or n % tn or k % tk or row_group % tm:
            continue
        windows = 2 * (tm * tk * a_bytes + tk * tn * b_bytes + tm * tn * out_bytes
                       + n_tile_extra * tm * tn * 4)
        temporaries = (3 if k > tk else 2) * tm * tn * 4
        need = windows + temporaries
        if need + COMPILER_SCRATCH <= VMEM_BUDGET:
            return tm, tn, tk, need
    raise ValueError(f"no matmul tiling for {(m, k, n)}")


def _identity(acc):
    return acc


def _matmul(a, b, *, out_dtype, epilogue=_identity, tiles=(), rows=(), name):
    m, k = a.shape
    n = b.shape[1]
    tm, tn, tk, need = _mm_blocks(m, k, n, a.dtype.itemsize, b.dtype.itemsize,
                                  jnp.dtype(out_dtype).itemsize, len(tiles),
                                  math.gcd(m, *[rpg for _, rpg in rows]))
    nk = k // tk
    in_specs = [pl.BlockSpec((tm, tk), lambda i, j, kk: (i, kk)),
                pl.BlockSpec((tk, tn), lambda i, j, kk: (kk, j))]
    for _, col0 in tiles:
        assert col0 % tn == 0
        in_specs.append(pl.BlockSpec((tm, tn), functools.partial(
            lambda i, j, kk, off: (i, j + off), off=col0 // tn)))
    for _, rows_per_group in rows:
        assert rows_per_group % tm == 0
        in_specs.append(pl.BlockSpec((None, 1, tn), functools.partial(
            lambda i, j, kk, tpg: (i // tpg, 0, j), tpg=rows_per_group // tm)))
    scratch = [pltpu.VMEM((tm, tn), F32)] if nk > 1 else []
    return pl.pallas_call(
        functools.partial(_mm_body, nk=nk, n_extra=len(tiles) + len(rows), epilogue=epilogue),
        grid=(m // tm, n // tn, nk),
        in_specs=in_specs,
        out_specs=pl.BlockSpec((tm, tn), lambda i, j, kk: (i, j)),
        out_shape=jax.ShapeDtypeStruct((m, n), out_dtype),
        scratch_shapes=scratch,
        compiler_params=_params(("parallel", "parallel", "arbitrary"), need),
        name=name,
    )(a, b, *[t for t, _ in tiles], *[r for r, _ in rows])


def _ada_body(c_ref, w_ref, b_ref, o_ref):
    h = _silu(c_ref[...]).astype(BF16)
    o_ref[...] = jnp.dot(h, w_ref[...].astype(BF16), preferred_element_type=F32) + b_ref[...]


def _ada_mod(c, w_ada, b_ada):
    bsz, d = c.shape
    n = w_ada.shape[1]
    mp = _round_up(bsz, SUBLANE)
    c_pad = jnp.zeros((mp, d), F32).at[:bsz].set(c)
    tn = 512 if n % 512 == 0 else LANE
    need = 2 * (mp * d * 4 + d * tn * 4 + tn * 4 + mp * tn * 4) + d * tn * 2
    out = pl.pallas_call(
        _ada_body,
        grid=(n // tn,),
        in_specs=[pl.BlockSpec((mp, d), lambda j: (0, 0)),
                  pl.BlockSpec((d, tn), lambda j: (0, j)),
                  pl.BlockSpec((1, tn), lambda j: (0, j))],
        out_specs=pl.BlockSpec((mp, tn), lambda j: (0, j)),
        out_shape=jax.ShapeDtypeStruct((mp, n), F32),
        compiler_params=_params(("parallel",), need),
        name="ada_mod",
    )(c_pad, w_ada, b_ada.reshape(1, n))
    return out[:bsz]


def _norm_mod_body(x_ref, g_ref, scale_ref, scale_t_ref, shift_ref, shift_t_ref, o_ref, *, eps):
    y = _rms(x_ref[...], eps) * g_ref[...]
    scale = scale_ref[...] + scale_t_ref[...]
    shift = shift_ref[...] + shift_t_ref[...]
    o_ref[...] = (y * (1.0 + scale) + shift).astype(o_ref.dtype)


def _norm_mod(x2, g, scale, scale_t, shift, shift_t, *, seq, eps, name):
    m, d = x2.shape
    tm = 256
    assert seq % tm == 0
    tiles_per_seq = seq // tm
    need = 2 * (tm * d * 4 + tm * d * 2 + 5 * d * 4) + 4 * tm * d * 4
    per_seq = pl.BlockSpec((None, 1, d), lambda i: (i // tiles_per_seq, 0, 0))
    shared = pl.BlockSpec((1, d), lambda i: (0, 0))
    return pl.pallas_call(
        functools.partial(_norm_mod_body, eps=eps),
        grid=(m // tm,),
        in_specs=[pl.BlockSpec((tm, d), lambda i: (i, 0)), shared, per_seq, shared, per_seq, shared],
        out_specs=pl.BlockSpec((tm, d), lambda i: (i, 0)),
        out_shape=jax.ShapeDtypeStruct((m, d), BF16),
        compiler_params=_params(("parallel",), need),
        name=name,
    )(x2, g.reshape(1, d), scale, scale_t.reshape(1, d), shift, shift_t.reshape(1, d))


def _shift_rows(x, prev8, s):
    rolled = pltpu.roll(x, s, 0)
    row = lax.broadcasted_iota(jnp.int32, (SUBLANE, x.shape[1]), 0)
    head = jnp.where(row < s, pltpu.roll(prev8, s, 0), rolled[:SUBLANE])
    return jnp.concatenate([head, rolled[SUBLANE:]], axis=0)


def _causal_conv(x, prev8, w, b):
    kw = w.shape[0]
    y = x * w[kw - 1:kw, :] + b
    for s in range(1, kw):
        y = y + _shift_rows(x, prev8, s) * w[kw - 1 - s:kw - s, :]
    return y


def _up_conv_body(h_ref, wa_ref, wb_ref, cwa_ref, cwb_ref, cba_ref, cbb_ref, o_ref,
                  prev_a, prev_b, *, tiles_per_seq):
    i = pl.program_id(1)

    @pl.when(i % tiles_per_seq == 0)
    def _():
        prev_a[...] = jnp.zeros_like(prev_a)
        prev_b[...] = jnp.zeros_like(prev_b)

    h = h_ref[...]
    ua = jnp.dot(h, wa_ref[...], preferred_element_type=F32)
    ub = jnp.dot(h, wb_ref[...], preferred_element_type=F32)
    a = _causal_conv(ua, prev_a[...], cwa_ref[...], cba_ref[...])
    b = _causal_conv(ub, prev_b[...], cwb_ref[...], cbb_ref[...])
    o_ref[...] = (_silu(a) * b).astype(o_ref.dtype)
    prev_a[...] = ua[-SUBLANE:]
    prev_b[...] = ub[-SUBLANE:]


def _up_conv_gate(h, w_up, conv_w, conv_b, *, seq, name):
    m, d = h.shape
    f = w_up.shape[1] // 2
    tm, tn = 512, 512
    assert seq % tm == 0 and f % tn == 0
    nf = f // tn
    kw = conv_w.shape[0]
    need = 2 * (tm * d * 2 + 2 * d * tn * 2 + tm * tn * 2) + 8 * tm * tn * 4
    cb = conv_b.reshape(1, 2 * f)
    return pl.pallas_call(
        functools.partial(_up_conv_body, tiles_per_seq=seq // tm),
        grid=(nf, m // tm),
        in_specs=[pl.BlockSpec((tm, d), lambda j, i: (i, 0)),
                  pl.BlockSpec((d, tn), lambda j, i: (0, j)),
                  pl.BlockSpec((d, tn), lambda j, i: (0, j + nf)),
                  pl.BlockSpec((kw, tn), lambda j, i: (0, j)),
                  pl.BlockSpec((kw, tn), lambda j, i: (0, j + nf)),
                  pl.BlockSpec((1, tn), lambda j, i: (0, j)),
                  pl.BlockSpec((1, tn), lambda j, i: (0, j + nf))],
        out_specs=pl.BlockSpec((tm, tn), lambda j, i: (i, j)),
        out_shape=jax.ShapeDtypeStruct((m, f), BF16),
        scratch_shapes=[pltpu.VMEM((SUBLANE, tn), F32), pltpu.VMEM((SUBLANE, tn), F32)],
        compiler_params=_params(("parallel", "arbitrary"), need),
        name=name,
    )(h, w_up, w_up, conv_w, conv_w, cb, cb)


_NT = (((1,), (1,)), ((), ()))
_HI = lax.Precision.HIGHEST


def _softplus(x):
    return jnp.maximum(x, 0.0) + jnp.log1p(jnp.exp(-jnp.abs(x)))


def _ssd_body(z_ref, xs_ref, b_ref, c_ref, dt_ref, hpar_ref, cwx_ref, cwb_ref, cwc_ref,
              cbx_ref, cbb_ref, cbc_ref, ng_ref, o_ref, state_ref, px_ref, pb_ref, pc_ref,
              *, hg, hd, eps):
    g = pl.program_id(1)
    q, hp = xs_ref.shape
    hall = dt_ref.shape[1]
    assert 2 * hd == LANE and hp == hg * hd and hg % 2 == 0

    @pl.when(pl.program_id(2) == 0)
    def _():
        state_ref[...] = jnp.zeros_like(state_ref)
        px_ref[...] = jnp.zeros_like(px_ref)
        pb_ref[...] = jnp.zeros_like(pb_ref)
        pc_ref[...] = jnp.zeros_like(pc_ref)

    xs_raw, b_raw, c_raw = xs_ref[...], b_ref[...], c_ref[...]
    xs = _silu(_causal_conv(xs_raw, px_ref[...], cwx_ref[...], cbx_ref[...]))
    bm = _silu(_causal_conv(b_raw, pb_ref[...], cwb_ref[...], cbb_ref[...]))
    cm = _silu(_causal_conv(c_raw, pc_ref[...], cwc_ref[...], cbc_ref[...]))
    px_ref[...] = xs_raw[-SUBLANE:]
    pb_ref[...] = b_raw[-SUBLANE:]
    pc_ref[...] = c_raw[-SUBLANE:]

    hpar = hpar_ref[...]
    dt_all = _softplus(dt_ref[...] + hpar[0:1])
    da_all = dt_all * (-jnp.exp(hpar[1:2]))
    row = lax.broadcasted_iota(jnp.int32, (q, q), 0)
    col = lax.broadcasted_iota(jnp.int32, (q, q), 1)
    causal = col <= row
    acum_all = jnp.dot(causal.astype(F32), da_all, precision=_HI,
                       preferred_element_type=F32)

    head_of_chan = g * hg + lax.broadcasted_iota(jnp.int32, (hall, hp), 1) // hd
    expand = (lax.broadcasted_iota(jnp.int32, (hall, hp), 0) == head_of_chan).astype(F32)
    pick = (lax.broadcasted_iota(jnp.int32, (hg, hall), 1)
            == g * hg + lax.broadcasted_iota(jnp.int32, (hg, hall), 0)).astype(F32)
    dt_x = jnp.dot(dt_all, expand, precision=_HI, preferred_element_type=F32)
    acum_x = jnp.dot(acum_all, expand, precision=_HI, preferred_element_type=F32)
    d_x = jnp.dot(hpar, expand, precision=_HI, preferred_element_type=F32)[2:3]
    acum_t = lax.dot_general(pick, acum_all, _NT, precision=_HI,
                             preferred_element_type=F32)
    last_x = acum_x[q - 1:q]

    xdt = xs * dt_x
    xdt_b = xdt.astype(BF16)
    cm_b = cm.astype(BF16)
    cb = lax.dot_general(cm_b, bm.astype(BF16), _NT, preferred_element_type=F32)

    lane_head = lax.broadcasted_iota(jnp.int32, (q, hall), 1)
    pair_rows = lax.broadcasted_iota(jnp.int32, (2 * q, LANE), 0) < q
    pair_lanes = lax.broadcasted_iota(jnp.int32, (2 * q, LANE), 1) < hd
    y_parts = []
    for pair in range(hg // 2):
        ms = []
        for h in (2 * pair, 2 * pair + 1):
            a_col = jnp.sum(jnp.where(lane_head == g * hg + h, acum_all, 0.0), axis=1,
                            keepdims=True)
            seg = a_col - acum_t[h:h + 1, :]
            ms.append((cb * jnp.exp(jnp.where(causal, seg, -jnp.inf))).astype(BF16))
        xk = xdt_b[:, pair * LANE:(pair + 1) * LANE]
        rhs = jnp.where(pair_rows == pair_lanes, jnp.concatenate([xk, xk], axis=0),
                        jnp.zeros((), BF16))
        y_parts.append(jnp.dot(jnp.concatenate(ms, axis=1), rhs, preferred_element_type=F32))
    y = jnp.concatenate(y_parts, axis=1)

    state = state_ref[...]
    y = y + jnp.dot(cm_b, state.astype(BF16), preferred_element_type=F32) * jnp.exp(acum_x)
    xw = (xdt * jnp.exp(last_x - acum_x)).astype(BF16)
    state_ref[...] = state * jnp.exp(last_x) + jnp.dot(
        bm.T.astype(BF16), xw, preferred_element_type=F32)

    y = y + d_x * xs
    yg = y * _silu(z_ref[...])
    yg = yg * lax.rsqrt(jnp.mean(yg * yg, axis=-1, keepdims=True) + eps)
    o_ref[...] = (yg * ng_ref[...]).astype(o_ref.dtype)


def _ssd_branch(p_ssd, dt_raw, hpar, conv_w, conv_b, norm_g, *, cfg, name):
    bsz, seq, _ = p_ssd.shape
    g_n, n, hd, q = cfg.ssd_groups, cfg.ssd_state, cfg.ssd_head_dim, cfg.ssd_chunk
    d_inner = norm_g.shape[0]
    hp = d_inner // g_n
    hg = hp // hd
    hall = hpar.shape[1]
    kw = conv_w.shape[0]
    xs0 = d_inner // hp
    b0 = 2 * d_inner // n
    c0 = b0 + g_n
    cwx, cwb, cwc = (conv_w[:, :d_inner], conv_w[:, d_inner:d_inner + g_n * n],
                     conv_w[:, d_inner + g_n * n:])
    cb2 = conv_b.reshape(1, -1)
    cbx, cbb, cbc = cb2[:, :d_inner], cb2[:, d_inner:d_inner + g_n * n], cb2[:, d_inner + g_n * n:]
    need = 2 * (2 * q * hp * 4 + 2 * q * n * 4 + q * hall * 4 + q * hp * 2) + 24 * q * hp * 4
    return pl.pallas_call(
        functools.partial(_ssd_body, hg=hg, hd=hd, eps=cfg.eps),
        grid=(bsz, g_n, seq // q),
        in_specs=[
            pl.BlockSpec((None, q, hp), lambda b, g, c: (b, c, g)),
            pl.BlockSpec((None, q, hp), lambda b, g, c: (b, c, xs0 + g)),
            pl.BlockSpec((None, q, n), lambda b, g, c: (b, c, b0 + g)),
            pl.BlockSpec((None, q, n), lambda b, g, c: (b, c, c0 + g)),
            pl.BlockSpec((None, q, hall), lambda b, g, c: (b, c, 0)),
            pl.BlockSpec((SUBLANE, hall), lambda b, g, c: (0, 0)),
            pl.BlockSpec((kw, hp), lambda b, g, c: (0, g)),
            pl.BlockSpec((kw, n), lambda b, g, c: (0, g)),
            pl.BlockSpec((kw, n), lambda b, g, c: (0, g)),
            pl.BlockSpec((1, hp), lambda b, g, c: (0, g)),
            pl.BlockSpec((1, n), lambda b, g, c: (0, g)),
            pl.BlockSpec((1, n), lambda b, g, c: (0, g)),
            pl.BlockSpec((1, hp), lambda b, g, c: (0, g)),
        ],
        out_specs=pl.BlockSpec((None, q, hp), lambda b, g, c: (b, c, g)),
        out_shape=jax.ShapeDtypeStruct((bsz, seq, d_inner), BF16),
        scratch_shapes=[pltpu.VMEM((n, hp), F32), pltpu.VMEM((SUBLANE, hp), F32),
                        pltpu.VMEM((SUBLANE, n), F32), pltpu.VMEM((SUBLANE, n), F32)],
        compiler_params=_params(("parallel", "parallel", "arbitrary"), need),
        name=name,
    )(p_ssd, p_ssd, p_ssd, p_ssd, dt_raw, hpar, cwx, cwb, cwc, cbx, cbb, cbc,
      norm_g.reshape(1, d_inner))


def _dsa_prep_body(k_ref, v_ref, ki_ref, kg_ref, kig_ref, kn_ref, vt_ref, kin_ref, *, hd, eps):
    for g in range(k_ref.shape[1] // hd):
        sl = slice(g * hd, (g + 1) * hd)
        kn_ref[:, sl] = (_rms(k_ref[:, sl], eps) * kg_ref[...]).astype(kn_ref.dtype)
        vt_ref[sl, :] = v_ref[:, sl].T.astype(vt_ref.dtype)
    kin_ref[...] = (_rms(ki_ref[...], eps) * kig_ref[...]).astype(kin_ref.dtype)


def _dsa_prep(p_att, p_small, k_norm_g, ki_norm_g, *, cfg, k_blk, v_blk, ki_blk, name):
    bsz, seq, _ = p_att.shape
    hd, idim, tk = cfg.att_head_dim, cfg.idx_dim, cfg.q_block
    kvw = cfg.att_kv_groups * hd
    nb = seq // tk
    need = 2 * (2 * tk * kvw * 4 + tk * idim * 4 + 2 * tk * kvw * 2 + tk * idim * 2) + 4 * tk * kvw * 4
    return pl.pallas_call(
        functools.partial(_dsa_prep_body, hd=hd, eps=cfg.eps),
        grid=(bsz, nb),
        in_specs=[pl.BlockSpec((None, tk, kvw), lambda b, i: (b, i, k_blk)),
                  pl.BlockSpec((None, tk, kvw), lambda b, i: (b, i, v_blk)),
                  pl.BlockSpec((None, tk, idim), lambda b, i: (b, i, ki_blk)),
                  pl.BlockSpec((1, hd), lambda b, i: (0, 0)),
                  pl.BlockSpec((1, idim), lambda b, i: (0, 0))],
        out_specs=[pl.BlockSpec((None, tk, kvw), lambda b, i: (b, i, 0)),
                   pl.BlockSpec((None, None, kvw, tk), lambda b, i: (b, i, 0, 0)),
                   pl.BlockSpec((None, tk, idim), lambda b, i: (b, i, 0))],
        out_shape=[jax.ShapeDtypeStruct((bsz, seq, kvw), BF16),
                   jax.ShapeDtypeStruct((bsz, nb, kvw, tk), BF16),
                   jax.ShapeDtypeStruct((bsz, seq, idim), BF16)],
        compiler_params=_params(("parallel", "parallel"), need),
        name=name,
    )(p_att, p_att, p_small, k_norm_g.reshape(1, hd), ki_norm_g.reshape(1, idim))


def _dsa_body(q_ref, qi_ref, wi_ref, kn_ref, vt_ref, kin_ref, qg_ref, o_ref,
              qn_s, qib_s, key_s, m_s, l_s, acc_s, *, nh, hd, kvg, ih, idim, ksel, eps):
    i = pl.program_id(1)
    tq = q_ref.shape[0]
    tk = tq
    hpg = nh // kvg

    for h in range(nh):
        qh = _rms(q_ref[:, h * hd:(h + 1) * hd], eps) * qg_ref[...] * (hd ** -0.5)
        qn_s[h * tq:(h + 1) * tq, :] = qh.astype(BF16)
    for h in range(ih):
        qib_s[h * tq:(h + 1) * tq, :] = qi_ref[:, h * idim:(h + 1) * idim].astype(BF16)
    wi_t = wi_ref[...].T * (idim ** -0.5 * ih ** -0.5)

    kpos = lax.broadcasted_iota(jnp.int32, (tk, tq), 0)
    qpos = lax.broadcasted_iota(jnp.int32, (tk, tq), 1)
    future = kpos > qpos

    def score_blk(kb, carry):
        ki_blk = kin_ref[pl.ds(pl.multiple_of(kb * tk, tk), tk), :]
        acc = jnp.zeros((tk, tq), F32)
        for h in range(ih):
            lt = lax.dot_general(ki_blk, qib_s[h * tq:(h + 1) * tq, :], _NT,
                                 preferred_element_type=F32)
            acc = acc + wi_t[h:h + 1, :] * jnp.maximum(lt, 0.0)
        bits = pltpu.bitcast(acc, jnp.int32)
        key_s[kb] = bits ^ ((bits >> 31) & jnp.int32(0x7FFFFFFF))
        return carry

    lax.fori_loop(0, i + 1, score_blk, 0)
    key_s[i] = jnp.where(future, jnp.int32(INT_MIN), key_s[i])

    def bit_step(bi, thr):
        cand = thr + (jnp.int32(1) << (31 - bi))

        def count_blk(kb, cnt):
            return cnt + jnp.where(key_s[kb] >= cand, 1.0, 0.0)

        cnt = lax.fori_loop(0, i + 1, count_blk, jnp.zeros((tk, tq), F32))
        return jnp.where(jnp.sum(cnt, axis=0, keepdims=True) >= ksel, cand, thr)

    thr = lax.fori_loop(0, 32, bit_step, jnp.full((1, tq), INT_MIN, jnp.int32))

    def above_blk(kb, cnt):
        return cnt + jnp.where(key_s[kb] > thr, 1.0, 0.0)

    n_above = lax.fori_loop(0, i + 1, above_blk, jnp.zeros((tk, tq), F32))
    ties_wanted = ksel - jnp.sum(n_above, axis=0, keepdims=True)
    earlier_key = (lax.broadcasted_iota(jnp.int32, (tk, tk), 1)
                   < lax.broadcasted_iota(jnp.int32, (tk, tk), 0)).astype(BF16)

    m_s[...] = jnp.full_like(m_s, NEG_BIG)
    l_s[...] = jnp.zeros_like(l_s)
    acc_s[...] = jnp.zeros_like(acc_s)

    def att_blk(kb, ties_before):
        keys = key_s[kb]
        tie = jnp.where(keys == thr, 1.0, 0.0)
        tie_rank = ties_before + jnp.dot(earlier_key, tie.astype(BF16),
                                         preferred_element_type=F32)
        keep_tie = jnp.where(tie_rank < ties_wanted, tie, 0.0)
        keep = jnp.where(keys > thr, 1.0, keep_tie)
        on_diag = (kb == i).astype(F32)
        keep = keep - jnp.where(future, on_diag, 0.0)
        bias = jnp.where(keep > 0.5, 0.0, NEG_BIG)
        row0 = pl.multiple_of(kb * tk, tk)
        for g in range(kvg):
            k_blk = kn_ref[pl.ds(row0, tk), g * hd:(g + 1) * hd]
            vt_blk = vt_ref[kb, g * hd:(g + 1) * hd, :]
            for h in range(g * hpg, (g + 1) * hpg):
                s = lax.dot_general(k_blk, qn_s[h * tq:(h + 1) * tq, :], _NT,
                                    preferred_element_type=F32) + bias
                m_old = m_s[h:h + 1, :]
                m_new = jnp.maximum(m_old, jnp.max(s, axis=0, keepdims=True))
                p = jnp.exp(s - m_new)
                alpha = jnp.exp(m_old - m_new)
                l_s[h:h + 1, :] = alpha * l_s[h:h + 1, :] + jnp.sum(p, axis=0, keepdims=True)
                acc_s[h * hd:(h + 1) * hd, :] = alpha * acc_s[h * hd:(h + 1) * hd, :] + jnp.dot(
                    vt_blk, p.astype(BF16), preferred_element_type=F32)
                m_s[h:h + 1, :] = m_new
        return ties_before + jnp.sum(tie, axis=0, keepdims=True)

    lax.fori_loop(0, i + 1, att_blk, jnp.zeros((1, tq), F32))

    for h in range(nh):
        o_t = acc_s[h * hd:(h + 1) * hd, :] / l_s[h:h + 1, :]
        o_ref[:, h * hd:(h + 1) * hd] = o_t.T.astype(o_ref.dtype)


def _dsa_attend(p_att, p_small, kn, vt, kin, q_norm_g, *, cfg, wi_blk, name):
    bsz, seq, _ = p_att.shape
    nh, hd, kvg = cfg.att_heads, cfg.att_head_dim, cfg.att_kv_groups
    ih, idim, tq = cfg.idx_heads, cfg.idx_dim, cfg.q_block
    assert nh * hd == ih * idim and ih <= LANE
    aw, kvw, nb = nh * hd, kvg * hd, seq // tq
    ksel = min(cfg.topk_max, seq // 4)
    need = (2 * (2 * tq * aw * 4 + tq * LANE * 4 + 2 * seq * kvw * 2 + seq * idim * 2 + tq * aw * 2)
            + 2 * nh * tq * hd * 2 + seq * tq * 4 + nh * hd * tq * 4 + 16 * tq * tq * 4)
    return pl.pallas_call(
        functools.partial(_dsa_body, nh=nh, hd=hd, kvg=kvg, ih=ih, idim=idim, ksel=float(ksel),
                          eps=cfg.eps),
        grid=(bsz, nb),
        in_specs=[pl.BlockSpec((None, tq, aw), lambda b, i: (b, i, 0)),
                  pl.BlockSpec((None, tq, aw), lambda b, i: (b, i, 1)),
                  pl.BlockSpec((None, tq, LANE), lambda b, i: (b, i, wi_blk)),
                  pl.BlockSpec((None, seq, kvw), lambda b, i: (b, 0, 0)),
                  pl.BlockSpec((None, nb, kvw, tq), lambda b, i: (b, 0, 0, 0)),
                  pl.BlockSpec((None, seq, idim), lambda b, i: (b, 0, 0)),
                  pl.BlockSpec((1, hd), lambda b, i: (0, 0))],
        out_specs=pl.BlockSpec((None, tq, aw), lambda b, i: (b, i, 0)),
        out_shape=jax.ShapeDtypeStruct((bsz, seq, aw), BF16),
        scratch_shapes=[pltpu.VMEM((nh * tq, hd), BF16), pltpu.VMEM((ih * tq, idim), BF16),
                        pltpu.VMEM((nb, tq, tq), jnp.int32),
                        pltpu.VMEM((nh, tq), F32), pltpu.VMEM((nh, tq), F32),
                        pltpu.VMEM((nh * hd, tq), F32)],
        compiler_params=_params(("parallel", "arbitrary"), need),
        name=name,
    )(p_att, p_att, p_small, kn, vt, kin, q_norm_g.reshape(1, hd))


def _pad_cols(w, width):
    return jnp.pad(w, ((0, 0), (0, width - w.shape[1])))


def _pack_w_in(w, cfg, d_model):
    d_inner = cfg.ssd_expand * d_model
    conv_ch = d_inner + 2 * cfg.ssd_groups * cfg.ssd_state
    heads = d_inner // cfg.ssd_head_dim
    aw = cfg.att_heads * cfg.att_head_dim
    kvw = cfg.att_kv_groups * cfg.att_head_dim
    sizes = (d_inner, conv_ch, heads, aw, kvw, kvw, cfg.idx_heads * cfg.idx_dim, cfg.idx_dim,
             cfg.idx_heads, d_model, d_model)
    assert sum(sizes) == w.shape[1]
    parts, o = [], 0
    for s in sizes:
        parts.append(w[:, o:o + s])
        o += s
    z, xbc, dt, q, k, v, qi, ki, wi, g_ssd, g_att = parts
    hall = _round_up(heads, LANE)
    cat = lambda ws: jnp.concatenate(ws, axis=1).astype(BF16)
    return (cat([z, xbc]), cat([q, qi, k, v]), cat([g_ssd, g_att]),
            cat([_pad_cols(dt, hall), ki, _pad_cols(wi, LANE)]), hall)


def _gated(acc, g):
    return _sigmoid(g) * acc


def _gated_add(acc, g, prev):
    return prev + _sigmoid(g) * acc


def _residual(acc, x, gate, gate_t):
    return x + (gate + gate_t) * acc


def _forward(cfg, x, c, w_ada, b_ada, ada_table, norm1_g, w_in, ssd_conv_w, ssd_conv_b,
             ssd_dt_bias, ssd_a_log, ssd_d, ssd_norm_g, w_ssd_out, q_norm_g, k_norm_g,
             idx_k_norm_g, w_att_out, w_o, norm2_g, w_up, ffn_conv_w, ffn_conv_b, w_down):
    bsz, seq, d = x.shape
    m = bsz * seq
    depth = w_in.shape[0]
    aw = cfg.att_heads * cfg.att_head_dim
    kvw = cfg.att_kv_groups * cfg.att_head_dim
    heads = cfg.ssd_expand * d // cfg.ssd_head_dim

    mod = _ada_mod(c, w_ada, b_ada).reshape(bsz, cfg.n_mod, d)
    shift1, scale1, gate1, shift2, scale2, gate2 = [mod[:, i:i + 1, :] for i in range(cfg.n_mod)]
    x2 = x.reshape(m, d)
    for l in range(depth):
        tab = ada_table[l]
        w_ssd, w_att, w_gate, w_small, hall = _pack_w_in(w_in[l], cfg, d)
        hpar = jnp.zeros((SUBLANE, hall), F32)
        hpar = hpar.at[0, :heads].set(ssd_dt_bias[l]).at[1, :heads].set(ssd_a_log[l])
        hpar = hpar.at[2, :heads].set(ssd_d[l])

        h = _norm_mod(x2, norm1_g[l], scale1, tab[1], shift1, tab[0], seq=seq, eps=cfg.eps,
                      name=f"norm1_l{l}")
        p_ssd = _matmul(h, w_ssd, out_dtype=F32, name=f"in_ssd_l{l}").reshape(bsz, seq, -1)
        p_att = _matmul(h, w_att, out_dtype=F32, name=f"in_att_l{l}").reshape(bsz, seq, -1)
        p_gate = _matmul(h, w_gate, out_dtype=F32, name=f"in_gate_l{l}")
        p_small = _matmul(h, w_small, out_dtype=F32, name=f"in_small_l{l}").reshape(bsz, seq, -1)

        y_ssd = _ssd_branch(p_ssd, p_small, hpar, ssd_conv_w[l], ssd_conv_b[l], ssd_norm_g[l],
                            cfg=cfg, name=f"ssd_l{l}")
        kn, vt, kin = _dsa_prep(p_att, p_small, k_norm_g[l], idx_k_norm_g[l], cfg=cfg,
                                k_blk=2 * aw // kvw, v_blk=2 * aw // kvw + 1,
                                ki_blk=hall // cfg.idx_dim, name=f"dsa_prep_l{l}")
        o_att = _dsa_attend(p_att, p_small, kn, vt, kin, q_norm_g[l], cfg=cfg,
                            wi_blk=(hall + cfg.idx_dim) // LANE, name=f"dsa_l{l}")

        part = _matmul(y_ssd.reshape(m, -1), w_ssd_out[l].astype(BF16), out_dtype=F32,
                       epilogue=_gated, tiles=[(p_gate, 0)], name=f"ssd_out_l{l}")
        merged = _matmul(o_att.reshape(m, -1), w_att_out[l].astype(BF16), out_dtype=BF16,
                         epilogue=_gated_add, tiles=[(p_gate, d), (part, 0)],
                         name=f"att_out_l{l}")
        x2 = _matmul(merged, w_o[l].astype(BF16), out_dtype=F32, epilogue=_residual,
                     tiles=[(x2, 0)], rows=[(gate1, seq), (tab[2].reshape(1, 1, d), m)],
                     name=f"mix_out_l{l}")

        h = _norm_mod(x2, norm2_g[l], scale2, tab[4], shift2, tab[3], seq=seq, eps=cfg.eps,
                      name=f"norm2_l{l}")
        act = _up_conv_gate(h, w_up[l].astype(BF16), ffn_conv_w[l], ffn_conv_b[l], seq=seq,
                            name=f"ffn_up_l{l}")
        x2 = _matmul(act, w_down[l].astype(BF16), out_dtype=F32, epilogue=_residual,
                     tiles=[(x2, 0)], rows=[(gate2, seq), (tab[5].reshape(1, 1, d), m)],
                     name=f"ffn_down_l{l}")
    return x2.reshape(bsz, seq, d)


def kernel(x, c, w_ada, b_ada, ada_table, norm1_g, w_in, ssd_conv_w, ssd_conv_b, ssd_dt_bias,
           ssd_a_log, ssd_d, ssd_norm_g, w_ssd_out, q_norm_g, k_norm_g, idx_k_norm_g, w_att_out,
           w_o, norm2_g, w_up, ffn_conv_w, ffn_conv_b, w_down):
    return _forward(Cfg(), x, c, w_ada, b_ada, ada_table, norm1_g, w_in, ssd_conv_w, ssd_conv_b,
                    ssd_dt_bias, ssd_a_log, ssd_d, ssd_norm_g, w_ssd_out, q_norm_g, k_norm_g,
                    idx_k_norm_g, w_att_out, w_o, norm2_g, w_up, ffn_conv_w, ffn_conv_b, w_down)
```

```python
import functools
import math
from typing import NamedTuple

import jax
import jax.numpy as jnp
from jax import lax
from jax.experimental import pallas as pl
from jax.experimental.pallas import tpu as pltpu

V7X_VMEM_BYTES = 64 * 1024 * 1024
LANE = 128
SUBLANE = 8
VMEM_BUDGET = V7X_VMEM_BYTES - 8 * 1024 * 1024
COMPILER_SCRATCH = 4 * 1024 * 1024

F32 = jnp.float32
BF16 = jnp.bfloat16
NEG_BIG = -1e30
LOG2_E = math.log2(math.e)
INT_MIN = -(2 ** 31)


class Cfg(NamedTuple):
    ssd_head_dim: int = 64
    ssd_groups: int = 8
    ssd_state: int = 128
    ssd_conv: int = 4
    ssd_chunk: int = 128
    ssd_expand: int = 2
    att_heads: int = 32
    att_head_dim: int = 128
    att_kv_groups: int = 4
    idx_heads: int = 32
    idx_dim: int = 128
    topk_max: int = 256
    q_block: int = 128
    ffn_mult: int = 2
    ffn_conv: int = 3
    n_mod: int = 6
    eps: float = 1e-6


def _round_up(x, m):
    return (x + m - 1) // m * m


def _params(sem, vmem_bytes):
    return pltpu.CompilerParams(dimension_semantics=sem,
                                vmem_limit_bytes=int(min(vmem_bytes + COMPILER_SCRATCH, VMEM_BUDGET)))


def _sigmoid(x):
    return 0.5 * jnp.tanh(0.5 * x) + 0.5


def _silu(x):
    return x * _sigmoid(x)


def _split3(x):
    hi = x.astype(BF16)
    rest = x - hi.astype(F32)
    mid = rest.astype(BF16)
    return hi, mid, (rest - mid.astype(F32)).astype(BF16)


def _rms(x, eps):
    return x * lax.rsqrt(jnp.mean(x * x, axis=-1, keepdims=True) + eps)


def _mm_body(*refs, nk, n_extra, epilogue):
    a_ref, b_ref = refs[0], refs[1]
    extra = refs[2:2 + n_extra]
    o_ref = refs[2 + n_extra]
    part = jnp.dot(a_ref[...], b_ref[...], preferred_element_type=F32)
    if nk == 1:
        o_ref[...] = epilogue(part, *[e[...] for e in extra]).astype(o_ref.dtype)
        return
    acc_ref = refs[3 + n_extra]
    k = pl.program_id(2)

    @pl.when(k == 0)
    def _():
        acc_ref[...] = part

    @pl.when(k > 0)
    def _():
        acc_ref[...] += part

    @pl.when(k == nk - 1)
    def _():
        o_ref[...] = epilogue(acc_ref[...], *[e[...] for e in extra]).astype(o_ref.dtype)


def _mm_blocks(m, k, n, a_bytes, b_bytes, out_bytes, n_tile_extra, row_group):
    for tm, tn, tk in ((1024, 1024, 4096), (1024, 1024, 2048), (1024, 512, 4096),
                       (512, 512, 4096), (512, 512, 2048), (1024, 128, 4096), (512, 256, 2048),
                       (256, 256, 2048), (256, 128, 1024), (128, 128, 512)):
        tk = min(tk, k)
        if m % tm or n % tn or k % tk or row_group % tm:
            continue
        windows = 2 * (tm * tk * a_bytes + tk * tn * b_bytes + tm * tn * out_bytes
                       + n_tile_extra * tm * tn * 4)
        temporaries = (3 if k > tk else 2) * tm * tn * 4
        need = windows + temporaries
        if need + COMPILER_SCRATCH <= VMEM_BUDGET:
            return tm, tn, tk, need
    raise ValueError(f"no matmul tiling for {(m, k, n)}")


def _identity(acc):
    return acc


def _matmul(a, b, *, out_dtype, epilogue=_identity, tiles=(), rows=(), name):
    m, k = a.shape
    n = b.shape[1]
    tm, tn, tk, need = _mm_blocks(m, k, n, a.dtype.itemsize, b.dtype.itemsize,
                                  jnp.dtype(out_dtype).itemsize, len(tiles),
                                  math.gcd(m, *[rpg for _, rpg in rows]))
    nk = k // tk
    in_specs = [pl.BlockSpec((tm, tk), lambda i, j, kk: (i, kk)),
                pl.BlockSpec((tk, tn), lambda i, j, kk: (kk, j))]
    for _, col0 in tiles:
        assert col0 % tn == 0
        in_specs.append(pl.BlockSpec((tm, tn), functools.partial(
            lambda i, j, kk, off: (i, j + off), off=col0 // tn)))
    for _, rows_per_group in rows:
        assert rows_per_group % tm == 0
        in_specs.append(pl.BlockSpec((None, 1, tn), functools.partial(
            lambda i, j, kk, tpg: (i // tpg, 0, j), tpg=rows_per_group // tm)))
    scratch = [pltpu.VMEM((tm, tn), F32)] if nk > 1 else []
    return pl.pallas_call(
        functools.partial(_mm_body, nk=nk, n_extra=len(tiles) + len(rows), epilogue=epilogue),
        grid=(m // tm, n // tn, nk),
        in_specs=in_specs,
        out_specs=pl.BlockSpec((tm, tn), lambda i, j, kk: (i, j)),
        out_shape=jax.ShapeDtypeStruct((m, n), out_dtype),
        scratch_shapes=scratch,
        compiler_params=_params(("parallel", "parallel", "arbitrary"), need),
        name=name,
    )(a, b, *[t for t, _ in tiles], *[r for r, _ in rows])


def _ada_body(c_ref, w_ref, b_ref, o_ref):
    h = _silu(c_ref[...]).astype(BF16)
    o_ref[...] = jnp.dot(h, w_ref[...].astype(BF16), preferred_element_type=F32) + b_ref[...]


def _ada_mod(c, w_ada, b_ada):
    bsz, d = c.shape
    n = w_ada.shape[1]
    mp = _round_up(bsz, SUBLANE)
    c_pad = jnp.zeros((mp, d), F32).at[:bsz].set(c)
    tn = 512 if n % 512 == 0 else LANE
    need = 2 * (mp * d * 4 + d * tn * 4 + tn * 4 + mp * tn * 4) + d * tn * 2
    out = pl.pallas_call(
        _ada_body,
        grid=(n // tn,),
        in_specs=[pl.BlockSpec((mp, d), lambda j: (0, 0)),
                  pl.BlockSpec((d, tn), lambda j: (0, j)),
                  pl.BlockSpec((1, tn), lambda j: (0, j))],
        out_specs=pl.BlockSpec((mp, tn), lambda j: (0, j)),
        out_shape=jax.ShapeDtypeStruct((mp, n), F32),
        compiler_params=_params(("parallel",), need),
        name="ada_mod",
    )(c_pad, w_ada, b_ada.reshape(1, n))
    return out[:bsz]


def _norm_mod_body(x_ref, g_ref, scale_ref, scale_t_ref, shift_ref, shift_t_ref, o_ref, *, eps):
    y = _rms(x_ref[...], eps) * g_ref[...]
    scale = scale_ref[...] + scale_t_ref[...]
    shift = shift_ref[...] + shift_t_ref[...]
    o_ref[...] = (y * (1.0 + scale) + shift).astype(o_ref.dtype)


def _norm_mod(x2, g, scale, scale_t, shift, shift_t, *, seq, eps, name):
    m, d = x2.shape
    tm = 256
    assert seq % tm == 0
    tiles_per_seq = seq // tm
    need = 2 * (tm * d * 4 + tm * d * 2 + 5 * d * 4) + 4 * tm * d * 4
    per_seq = pl.BlockSpec((None, 1, d), lambda i: (i // tiles_per_seq, 0, 0))
    shared = pl.BlockSpec((1, d), lambda i: (0, 0))
    return pl.pallas_call(
        functools.partial(_norm_mod_body, eps=eps),
        grid=(m // tm,),
        in_specs=[pl.BlockSpec((tm, d), lambda i: (i, 0)), shared, per_seq, shared, per_seq, shared],
        out_specs=pl.BlockSpec((tm, d), lambda i: (i, 0)),
        out_shape=jax.ShapeDtypeStruct((m, d), BF16),
        compiler_params=_params(("parallel",), need),
        name=name,
    )(x2, g.reshape(1, d), scale, scale_t.reshape(1, d), shift, shift_t.reshape(1, d))


def _shift_rows(x, prev8, s):
    rolled = pltpu.roll(x, s, 0)
    row = lax.broadcasted_iota(jnp.int32, (SUBLANE, x.shape[1]), 0)
    head = jnp.where(row < s, pltpu.roll(prev8, s, 0), rolled[:SUBLANE])
    return jnp.concatenate([head, rolled[SUBLANE:]], axis=0)


def _causal_conv(x, prev8, w, b):
    kw = w.shape[0]
    y = x * w[kw - 1:kw, :] + b
    for s in range(1, kw):
        y = y + _shift_rows(x, prev8, s) * w[kw - 1 - s:kw - s, :]
    return y


def _up_conv_body(h_ref, wa_ref, wb_ref, cwa_ref, cwb_ref, cba_ref, cbb_ref, o_ref,
                  prev_a, prev_b, *, tiles_per_seq):
    i = pl.program_id(1)

    @pl.when(i % tiles_per_seq == 0)
    def _():
        prev_a[...] = jnp.zeros_like(prev_a)
        prev_b[...] = jnp.zeros_like(prev_b)

    h = h_ref[...]
    ua = jnp.dot(h, wa_ref[...], preferred_element_type=F32)
    ub = jnp.dot(h, wb_ref[...], preferred_element_type=F32)
    a = _causal_conv(ua, prev_a[...], cwa_ref[...], cba_ref[...])
    b = _causal_conv(ub, prev_b[...], cwb_ref[...], cbb_ref[...])
    o_ref[...] = (_silu(a) * b).astype(o_ref.dtype)
    prev_a[...] = ua[-SUBLANE:]
    prev_b[...] = ub[-SUBLANE:]


def _up_conv_gate(h, w_up, conv_w, conv_b, *, seq, name):
    m, d = h.shape
    f = w_up.shape[1] // 2
    tm, tn = 512, 512
    assert seq % tm == 0 and f % tn == 0
    nf = f // tn
    kw = conv_w.shape[0]
    need = 2 * (tm * d * 2 + 2 * d * tn * 2 + tm * tn * 2) + 8 * tm * tn * 4
    cb = conv_b.reshape(1, 2 * f)
    return pl.pallas_call(
        functools.partial(_up_conv_body, tiles_per_seq=seq // tm),
        grid=(nf, m // tm),
        in_specs=[pl.BlockSpec((tm, d), lambda j, i: (i, 0)),
                  pl.BlockSpec((d, tn), lambda j, i: (0, j)),
                  pl.BlockSpec((d, tn), lambda j, i: (0, j + nf)),
                  pl.BlockSpec((kw, tn), lambda j, i: (0, j)),
                  pl.BlockSpec((kw, tn), lambda j, i: (0, j + nf)),
                  pl.BlockSpec((1, tn), lambda j, i: (0, j)),
                  pl.BlockSpec((1, tn), lambda j, i: (0, j + nf))],
        out_specs=pl.BlockSpec((tm, tn), lambda j, i: (i, j)),
        out_shape=jax.ShapeDtypeStruct((m, f), BF16),
        scratch_shapes=[pltpu.VMEM((SUBLANE, tn), F32), pltpu.VMEM((SUBLANE, tn), F32)],
        compiler_params=_params(("parallel", "arbitrary"), need),
        name=name,
    )(h, w_up, w_up, conv_w, conv_w, cb, cb)


_NT = (((1,), (1,)), ((), ()))


def _softplus(x):
    return jnp.maximum(x, 0.0) + jnp.log1p(jnp.exp(-jnp.abs(x)))


def _conv_from_history(ext_ref, w, b):
    t = ext_ref.shape[0] - SUBLANE
    kw = w.shape[0]
    y = ext_ref[SUBLANE:, :] * w[kw - 1:kw, :] + b
    for s in range(1, kw):
        y = y + ext_ref[SUBLANE - s:SUBLANE - s + t, :] * w[kw - 1 - s:kw - s, :]
    return y


def _ssd_body(z_ref, xs_ref, b_ref, c_ref, dt_ref, hpar_ref, cwx_ref, cwb_ref, cwc_ref,
              cbx_ref, cbb_ref, cbc_ref, ng_ref, o_ref, state_ref, ex_ref, eb_ref, ec_ref,
              *, hg, hd, eps):
    g = pl.program_id(1)
    q, hp = xs_ref.shape
    hall = dt_ref.shape[1]
    assert 2 * hd == LANE and hp == hg * hd and hg % 2 == 0

    @pl.when(pl.program_id(2) == 0)
    def _():
        state_ref[...] = jnp.zeros_like(state_ref)
        for ext in (ex_ref, eb_ref, ec_ref):
            ext[:SUBLANE, :] = jnp.zeros((SUBLANE, ext.shape[1]), F32)

    ex_ref[SUBLANE:, :] = xs_ref[...]
    eb_ref[SUBLANE:, :] = b_ref[...]
    ec_ref[SUBLANE:, :] = c_ref[...]
    xs = _silu(_conv_from_history(ex_ref, cwx_ref[...], cbx_ref[...]))
    bm = _silu(_conv_from_history(eb_ref, cwb_ref[...], cbb_ref[...]))
    cm = _silu(_conv_from_history(ec_ref, cwc_ref[...], cbc_ref[...]))
    for ext in (ex_ref, eb_ref, ec_ref):
        ext[:SUBLANE, :] = ext[q:, :]

    hpar = hpar_ref[...]
    dt_all = _softplus(dt_ref[...] + hpar[0:1])
    da_all = dt_all * (-jnp.exp(hpar[1:2]))
    row = lax.broadcasted_iota(jnp.int32, (q, q), 0)
    col = lax.broadcasted_iota(jnp.int32, (q, q), 1)
    causal = col <= row
    ac3 = jnp.dot(causal.astype(BF16), jnp.concatenate(_split3(da_all), axis=1),
                  preferred_element_type=F32)
    acum_all = ac3[:, :hall] + ac3[:, hall:2 * hall] + ac3[:, 2 * hall:]

    head_of_chan = g * hg + lax.broadcasted_iota(jnp.int32, (hall, hp), 1) // hd
    expand = (lax.broadcasted_iota(jnp.int32, (hall, hp), 0) == head_of_chan).astype(BF16)
    pick = (lax.broadcasted_iota(jnp.int32, (hg, hall), 1)
            == g * hg + lax.broadcasted_iota(jnp.int32, (hg, hall), 0)).astype(BF16)
    stack = jnp.concatenate([dt_all, acum_all, hpar, hpar], axis=0)
    ns = stack.shape[0]
    sp3 = jnp.dot(jnp.concatenate(_split3(stack), axis=0), expand, preferred_element_type=F32)
    spread = sp3[:ns] + sp3[ns:2 * ns] + sp3[2 * ns:]
    dt_x, acum_x, d_x = spread[:q], spread[q:2 * q], spread[2 * q + 2:2 * q + 3]
    at3 = lax.dot_general(pick, jnp.concatenate(_split3(acum_all), axis=0), _NT,
                          preferred_element_type=F32)
    acum_t = at3[:, :q] + at3[:, q:2 * q] + at3[:, 2 * q:]
    last_x = acum_x[q - 1:q]

    xdt = xs * dt_x
    xdt_b = xdt.astype(BF16)
    cm_b = cm.astype(BF16)
    cb = lax.dot_general(cm_b, bm.astype(BF16), _NT, preferred_element_type=F32)

    lane_head = lax.broadcasted_iota(jnp.int32, (q, hall), 1)
    pair_rows = lax.broadcasted_iota(jnp.int32, (2 * q, LANE), 0) < q
    pair_lanes = lax.broadcasted_iota(jnp.int32, (2 * q, LANE), 1) < hd
    y_parts = []
    for pair in range(hg // 2):
        ms = []
        for h in (2 * pair, 2 * pair + 1):
            a_col = jnp.sum(jnp.where(lane_head == g * hg + h, acum_all, 0.0), axis=1,
                            keepdims=True)
            seg = a_col - acum_t[h:h + 1, :]
            ms.append((cb * jnp.exp(jnp.where(causal, seg, -jnp.inf))).astype(BF16))
        xk = xdt_b[:, pair * LANE:(pair + 1) * LANE]
        rhs = jnp.where(pair_rows == pair_lanes, jnp.concatenate([xk, xk], axis=0),
                        jnp.zeros((), BF16))
        y_parts.append(jnp.dot(jnp.concatenate(ms, axis=1), rhs, preferred_element_type=F32))
    y = jnp.concatenate(y_parts, axis=1)

    state = state_ref[...]
    y = y + jnp.dot(cm_b, state.astype(BF16), preferred_element_type=F32) * jnp.exp(acum_x)
    xw = (xdt * jnp.exp(last_x - acum_x)).astype(BF16)
    state_ref[...] = state * jnp.exp(last_x) + jnp.dot(
        bm.T.astype(BF16), xw, preferred_element_type=F32)

    y = y + d_x * xs
    yg = y * _silu(z_ref[...])
    yg = yg * lax.rsqrt(jnp.mean(yg * yg, axis=-1, keepdims=True) + eps)
    o_ref[...] = (yg * ng_ref[...]).astype(o_ref.dtype)


def _ssd_branch(p_ssd, dt_raw, hpar, conv_w, conv_b, norm_g, *, cfg, name):
    bsz, seq, _ = p_ssd.shape
    g_n, n, hd, q = cfg.ssd_groups, cfg.ssd_state, cfg.ssd_head_dim, cfg.ssd_chunk
    d_inner = norm_g.shape[0]
    hp = d_inner // g_n
    hg = hp // hd
    hall = hpar.shape[1]
    kw = conv_w.shape[0]
    xs0 = d_inner // hp
    b0 = 2 * d_inner // n
    c0 = b0 + g_n
    cwx, cwb, cwc = (conv_w[:, :d_inner], conv_w[:, d_inner:d_inner + g_n * n],
                     conv_w[:, d_inner + g_n * n:])
    cb2 = conv_b.reshape(1, -1)
    cbx, cbb, cbc = cb2[:, :d_inner], cb2[:, d_inner:d_inner + g_n * n], cb2[:, d_inner + g_n * n:]
    need = 2 * (2 * q * hp * 4 + 2 * q * n * 4 + q * hall * 4 + q * hp * 2) + 24 * q * hp * 4
    return pl.pallas_call(
        functools.partial(_ssd_body, hg=hg, hd=hd, eps=cfg.eps),
        grid=(bsz, g_n, seq // q),
        in_specs=[
            pl.BlockSpec((None, q, hp), lambda b, g, c: (b, c, g)),
            pl.BlockSpec((None, q, hp), lambda b, g, c: (b, c, xs0 + g)),
            pl.BlockSpec((None, q, n), lambda b, g, c: (b, c, b0 + g)),
            pl.BlockSpec((None, q, n), lambda b, g, c: (b, c, c0 + g)),
            pl.BlockSpec((None, q, hall), lambda b, g, c: (b, c, 0)),
            pl.BlockSpec((SUBLANE, hall), lambda b, g, c: (0, 0)),
            pl.BlockSpec((kw, hp), lambda b, g, c: (0, g)),
            pl.BlockSpec((kw, n), lambda b, g, c: (0, g)),
            pl.BlockSpec((kw, n), lambda b, g, c: (0, g)),
            pl.BlockSpec((1, hp), lambda b, g, c: (0, g)),
            pl.BlockSpec((1, n), lambda b, g, c: (0, g)),
            pl.BlockSpec((1, n), lambda b, g, c: (0, g)),
            pl.BlockSpec((1, hp), lambda b, g, c: (0, g)),
        ],
        out_specs=pl.BlockSpec((None, q, hp), lambda b, g, c: (b, c, g)),
        out_shape=jax.ShapeDtypeStruct((bsz, seq, d_inner), BF16),
        scratch_shapes=[pltpu.VMEM((n, hp), F32), pltpu.VMEM((SUBLANE + q, hp), F32),
                        pltpu.VMEM((SUBLANE + q, n), F32), pltpu.VMEM((SUBLANE + q, n), F32)],
        compiler_params=_params(("parallel", "parallel", "arbitrary"), need),
        name=name,
    )(p_ssd, p_ssd, p_ssd, p_ssd, dt_raw, hpar, cwx, cwb, cwc, cbx, cbb, cbc,
      norm_g.reshape(1, d_inner))


def _dsa_prep_body(k_ref, v_ref, ki_ref, kg_ref, kig_ref, kn_ref, vt_ref, kin_ref, *, hd, eps):
    for g in range(k_ref.shape[1] // hd):
        sl = slice(g * hd, (g + 1) * hd)
        kn_ref[:, sl] = (_rms(k_ref[:, sl], eps) * kg_ref[...]).astype(kn_ref.dtype)
        vt_ref[sl, :] = v_ref[:, sl].T.astype(vt_ref.dtype)
    kin_ref[...] = (_rms(ki_ref[...], eps) * kig_ref[...]).astype(kin_ref.dtype)


def _dsa_prep(p_att, p_small, k_norm_g, ki_norm_g, *, cfg, k_blk, v_blk, ki_blk, name):
    bsz, seq, _ = p_att.shape
    hd, idim, tk = cfg.att_head_dim, cfg.idx_dim, cfg.q_block
    kvw = cfg.att_kv_groups * hd
    nb = seq // tk
    need = 2 * (2 * tk * kvw * 4 + tk * idim * 4 + 2 * tk * kvw * 2 + tk * idim * 2) + 4 * tk * kvw * 4
    return pl.pallas_call(
        functools.partial(_dsa_prep_body, hd=hd, eps=cfg.eps),
        grid=(bsz, nb),
        in_specs=[pl.BlockSpec((None, tk, kvw), lambda b, i: (b, i, k_blk)),
                  pl.BlockSpec((None, tk, kvw), lambda b, i: (b, i, v_blk)),
                  pl.BlockSpec((None, tk, idim), lambda b, i: (b, i, ki_blk)),
                  pl.BlockSpec((1, hd), lambda b, i: (0, 0)),
                  pl.BlockSpec((1, idim), lambda b, i: (0, 0))],
        out_specs=[pl.BlockSpec((None, tk, kvw), lambda b, i: (b, i, 0)),
                   pl.BlockSpec((None, None, kvw, tk), lambda b, i: (b, i, 0, 0)),
                   pl.BlockSpec((None, tk, idim), lambda b, i: (b, i, 0))],
        out_shape=[jax.ShapeDtypeStruct((bsz, seq, kvw), BF16),
                   jax.ShapeDtypeStruct((bsz, nb, kvw, tk), BF16),
                   jax.ShapeDtypeStruct((bsz, seq, idim), BF16)],
        compiler_params=_params(("parallel", "parallel"), need),
        name=name,
    )(p_att, p_att, p_small, k_norm_g.reshape(1, hd), ki_norm_g.reshape(1, idim))


def _dsa_body(q_ref, qi_ref, wi_ref, kn_ref, vt_ref, kin_ref, qg_ref, o_ref,
              qn_s, qib_s, key_s, m_s, l_s, acc_s, *, nh, hd, kvg, ih, idim, ksel, eps):
    i = pl.program_id(1)
    tq = q_ref.shape[0]
    tk = tq
    hpg = nh // kvg
    hq = hpg * tq
    assert ih % hpg == 0

    for h in range(nh):
        qh = _rms(q_ref[:, h * hd:(h + 1) * hd], eps) * qg_ref[...] * (hd ** -0.5 * LOG2_E)
        qn_s[:, h * tq:(h + 1) * tq] = qh.T.astype(BF16)
    for h in range(ih):
        qib_s[:, h * tq:(h + 1) * tq] = qi_ref[:, h * idim:(h + 1) * idim].T.astype(BF16)
    wi_t = wi_ref[...].T * (idim ** -0.5 * ih ** -0.5)

    kpos = lax.broadcasted_iota(jnp.int32, (tk, tq), 0)
    qpos = lax.broadcasted_iota(jnp.int32, (tk, tq), 1)
    future = kpos > qpos

    def score_blk(kb, carry):
        ki_blk = kin_ref[pl.ds(pl.multiple_of(kb * tk, tk), tk), :]
        acc = jnp.zeros((tk, tq), F32)
        for h0 in range(0, ih, hpg):
            lt = jnp.dot(ki_blk, qib_s[:, h0 * tq:(h0 + hpg) * tq],
                         preferred_element_type=F32)
            for h in range(h0, h0 + hpg):
                acc = acc + wi_t[h:h + 1, :] * jnp.maximum(lt[:, (h - h0) * tq:(h - h0 + 1) * tq], 0.0)
        bits = pltpu.bitcast(acc, jnp.int32)
        key_s[kb] = bits ^ ((bits >> 31) & jnp.int32(0x7FFFFFFF))
        return carry

    lax.fori_loop(0, i + 1, score_blk, 0)
    key_s[i] = jnp.where(future, jnp.int32(INT_MIN), key_s[i])

    def bit_step(bi, thr):
        cand = thr + (jnp.int32(1) << (31 - bi))

        def count_blk(kb, cnt):
            return cnt + jnp.where(key_s[kb] >= cand, 1.0, 0.0)

        cnt = lax.fori_loop(0, i + 1, count_blk, jnp.zeros((tk, tq), F32))
        return jnp.where(jnp.sum(cnt, axis=0, keepdims=True) >= ksel, cand, thr)

    thr = lax.fori_loop(0, 32, bit_step, jnp.full((1, tq), INT_MIN, jnp.int32))

    def above_blk(kb, cnt):
        return cnt + jnp.where(key_s[kb] > thr, 1.0, 0.0)

    n_above = lax.fori_loop(0, i + 1, above_blk, jnp.zeros((tk, tq), F32))
    ties_wanted = ksel - jnp.sum(n_above, axis=0, keepdims=True)
    earlier_key = (lax.broadcasted_iota(jnp.int32, (tk, tk), 1)
                   < lax.broadcasted_iota(jnp.int32, (tk, tk), 0)).astype(BF16)

    m_s[...] = jnp.full_like(m_s, NEG_BIG)
    l_s[...] = jnp.zeros_like(l_s)
    acc_s[...] = jnp.zeros_like(acc_s)

    def att_blk(kb, ties_before):
        keys = key_s[kb]
        tie = jnp.where(keys == thr, 1.0, 0.0)
        tie_rank = ties_before + jnp.dot(earlier_key, tie.astype(BF16),
                                         preferred_element_type=F32)
        keep_tie = jnp.where(tie_rank < ties_wanted, tie, 0.0)
        keep = jnp.where(keys > thr, 1.0, keep_tie)
        on_diag = (kb == i).astype(F32)
        keep = keep - jnp.where(future, on_diag, 0.0)
        bias = jnp.where(keep > 0.5, 0.0, NEG_BIG)
        bias = jnp.concatenate([bias] * hpg, axis=1)
        row0 = pl.multiple_of(kb * tk, tk)
        for g in range(kvg):
            k_blk = kn_ref[pl.ds(row0, tk), g * hd:(g + 1) * hd]
            vt_blk = vt_ref[kb, g * hd:(g + 1) * hd, :]
            s = jnp.dot(k_blk, qn_s[:, g * hq:(g + 1) * hq],
                        preferred_element_type=F32) + bias
            m_old = m_s[g:g + 1, :]
            m_new = jnp.maximum(m_old, jnp.max(s, axis=0, keepdims=True))
            p = jnp.exp2(s - m_new)
            alpha = jnp.exp2(m_old - m_new)
            l_s[g:g + 1, :] = alpha * l_s[g:g + 1, :] + jnp.sum(p, axis=0, keepdims=True)
            acc_s[g * hd:(g + 1) * hd, :] = alpha * acc_s[g * hd:(g + 1) * hd, :] + jnp.dot(
                vt_blk, p.astype(BF16), preferred_element_type=F32)
            m_s[g:g + 1, :] = m_new
        return ties_before + jnp.sum(tie, axis=0, keepdims=True)

    lax.fori_loop(0, i + 1, att_blk, jnp.zeros((1, tq), F32))

    for g in range(kvg):
        o_t = acc_s[g * hd:(g + 1) * hd, :] / l_s[g:g + 1, :]
        for hh in range(hpg):
            h = g * hpg + hh
            o_ref[:, h * hd:(h + 1) * hd] = o_t[:, hh * tq:(hh + 1) * tq].T.astype(o_ref.dtype)


def _dsa_attend(p_att, p_small, kn, vt, kin, q_norm_g, *, cfg, wi_blk, name):
    bsz, seq, _ = p_att.shape
    nh, hd, kvg = cfg.att_heads, cfg.att_head_dim, cfg.att_kv_groups
    ih, idim, tq = cfg.idx_heads, cfg.idx_dim, cfg.q_block
    assert nh * hd == ih * idim and ih <= LANE
    aw, kvw, nb, hpg = nh * hd, kvg * hd, seq // tq, nh // kvg
    ksel = min(cfg.topk_max, seq // 4)
    need = (2 * (2 * tq * aw * 4 + tq * LANE * 4 + 2 * seq * kvw * 2 + seq * idim * 2 + tq * aw * 2)
            + 2 * nh * tq * hd * 2 + seq * tq * 4 + nh * hd * tq * 4 + 16 * tq * tq * 4)
    return pl.pallas_call(
        functools.partial(_dsa_body, nh=nh, hd=hd, kvg=kvg, ih=ih, idim=idim, ksel=float(ksel),
                          eps=cfg.eps),
        grid=(bsz, nb),
        in_specs=[pl.BlockSpec((None, tq, aw), lambda b, i: (b, i, 0)),
                  pl.BlockSpec((None, tq, aw), lambda b, i: (b, i, 1)),
                  pl.BlockSpec((None, tq, LANE), lambda b, i: (b, i, wi_blk)),
                  pl.BlockSpec((None, seq, kvw), lambda b, i: (b, 0, 0)),
                  pl.BlockSpec((None, nb, kvw, tq), lambda b, i: (b, 0, 0, 0)),
                  pl.BlockSpec((None, seq, idim), lambda b, i: (b, 0, 0)),
                  pl.BlockSpec((1, hd), lambda b, i: (0, 0))],
        out_specs=pl.BlockSpec((None, tq, aw), lambda b, i: (b, i, 0)),
        out_shape=jax.ShapeDtypeStruct((bsz, seq, aw), BF16),
        scratch_shapes=[pltpu.VMEM((hd, nh * tq), BF16), pltpu.VMEM((idim, ih * tq), BF16),
                        pltpu.VMEM((nb, tq, tq), jnp.int32),
                        pltpu.VMEM((kvg, hpg * tq), F32), pltpu.VMEM((kvg, hpg * tq), F32),
                        pltpu.VMEM((kvg * hd, hpg * tq), F32)],
        compiler_params=_params(("parallel", "arbitrary"), need),
        name=name,
    )(p_att, p_att, p_small, kn, vt, kin, q_norm_g.reshape(1, hd))


def _pad_cols(w, width):
    return jnp.pad(w, ((0, 0), (0, width - w.shape[1])))


def _pack_w_in(w, cfg, d_model):
    d_inner = cfg.ssd_expand * d_model
    conv_ch = d_inner + 2 * cfg.ssd_groups * cfg.ssd_state
    heads = d_inner // cfg.ssd_head_dim
    aw = cfg.att_heads * cfg.att_head_dim
    kvw = cfg.att_kv_groups * cfg.att_head_dim
    sizes = (d_inner, conv_ch, heads, aw, kvw, kvw, cfg.idx_heads * cfg.idx_dim, cfg.idx_dim,
             cfg.idx_heads, d_model, d_model)
    assert sum(sizes) == w.shape[1]
    parts, o = [], 0
    for s in sizes:
        parts.append(w[:, o:o + s])
        o += s
    z, xbc, dt, q, k, v, qi, ki, wi, g_ssd, g_att = parts
    hall = _round_up(heads, LANE)
    cat = lambda ws: jnp.concatenate(ws, axis=1).astype(BF16)
    return (cat([z, xbc]), cat([q, qi, k, v]), cat([g_ssd, g_att]),
            cat([_pad_cols(dt, hall), ki, _pad_cols(wi, LANE)]), hall)


def _gated(acc, g):
    return _sigmoid(g) * acc


def _gated_add(acc, g, prev):
    return prev + _sigmoid(g) * acc


def _residual(acc, x, gate, gate_t):
    return x + (gate + gate_t) * acc


def _forward(cfg, x, c, w_ada, b_ada, ada_table, norm1_g, w_in, ssd_conv_w, ssd_conv_b,
             ssd_dt_bias, ssd_a_log, ssd_d, ssd_norm_g, w_ssd_out, q_norm_g, k_norm_g,
             idx_k_norm_g, w_att_out, w_o, norm2_g, w_up, ffn_conv_w, ffn_conv_b, w_down):
    bsz, seq, d = x.shape
    m = bsz * seq
    depth = w_in.shape[0]
    aw = cfg.att_heads * cfg.att_head_dim
    kvw = cfg.att_kv_groups * cfg.att_head_dim
    heads = cfg.ssd_expand * d // cfg.ssd_head_dim

    mod = _ada_mod(c, w_ada, b_ada).reshape(bsz, cfg.n_mod, d)
    shift1, scale1, gate1, shift2, scale2, gate2 = [mod[:, i:i + 1, :] for i in range(cfg.n_mod)]
    x2 = x.reshape(m, d)
    for l in range(depth):
        tab = ada_table[l]
        w_ssd, w_att, w_gate, w_small, hall = _pack_w_in(w_in[l], cfg, d)
        hpar = jnp.zeros((SUBLANE, hall), F32)
        hpar = hpar.at[0, :heads].set(ssd_dt_bias[l]).at[1, :heads].set(ssd_a_log[l])
        hpar = hpar.at[2, :heads].set(ssd_d[l])

        h = _norm_mod(x2, norm1_g[l], scale1, tab[1], shift1, tab[0], seq=seq, eps=cfg.eps,
                      name=f"norm1_l{l}")
        p_ssd = _matmul(h, w_ssd, out_dtype=F32, name=f"in_ssd_l{l}").reshape(bsz, seq, -1)
        p_att = _matmul(h, w_att, out_dtype=F32, name=f"in_att_l{l}").reshape(bsz, seq, -1)
        p_gate = _matmul(h, w_gate, out_dtype=F32, name=f"in_gate_l{l}")
        p_small = _matmul(h, w_small, out_dtype=F32, name=f"in_small_l{l}").reshape(bsz, seq, -1)

        y_ssd = _ssd_branch(p_ssd, p_small, hpar, ssd_conv_w[l], ssd_conv_b[l], ssd_norm_g[l],
                            cfg=cfg, name=f"ssd_l{l}")
        kn, vt, kin = _dsa_prep(p_att, p_small, k_norm_g[l], idx_k_norm_g[l], cfg=cfg,
                                k_blk=2 * aw // kvw, v_blk=2 * aw // kvw + 1,
                                ki_blk=hall // cfg.idx_dim, name=f"dsa_prep_l{l}")
        o_att = _dsa_attend(p_att, p_small, kn, vt, kin, q_norm_g[l], cfg=cfg,
                            wi_blk=(hall + cfg.idx_dim) // LANE, name=f"dsa_l{l}")

        part = _matmul(y_ssd.reshape(m, -1), w_ssd_out[l].astype(BF16), out_dtype=F32,
                       epilogue=_gated, tiles=[(p_gate, 0)], name=f"ssd_out_l{l}")
        merged = _matmul(o_att.reshape(m, -1), w_att_out[l].astype(BF16), out_dtype=BF16,
                         epilogue=_gated_add, tiles=[(p_gate, d), (part, 0)],
                         name=f"att_out_l{l}")
        x2 = _matmul(merged, w_o[l].astype(BF16), out_dtype=F32, epilogue=_residual,
                     tiles=[(x2, 0)], rows=[(gate1, seq), (tab[2].reshape(1, 1, d), m)],
                     name=f"mix_out_l{l}")

        h = _norm_mod(x2, norm2_g[l], scale2, tab[4], shift2, tab[3], seq=seq, eps=cfg.eps,
                      name=f"norm2_l{l}")
        act = _up_conv_gate(h, w_up[l].astype(BF16), ffn_conv_w[l], ffn_conv_b[l], seq=seq,
                            name=f"ffn_up_l{l}")
        x2 = _matmul(act, w_down[l].astype(BF16), out_dtype=F32, epilogue=_residual,
                     tiles=[(x2, 0)], rows=[(gate2, seq), (tab[5].reshape(1, 1, d), m)],
                     name=f"ffn_down_l{l}")
    return x2.reshape(bsz, seq, d)


def kernel(x, c, w_ada, b_ada, ada_table, norm1_g, w_in, ssd_conv_w, ssd_conv_b, ssd_dt_bias,
           ssd_a_log, ssd_d, ssd_norm_g, w_ssd_out, q_norm_g, k_norm_g, idx_k_norm_g, w_att_out,
           w_o, norm2_g, w_up, ffn_conv_w, ffn_conv_b, w_down):
    return _forward(Cfg(), x, c, w_ada, b_ada, ada_table, norm1_g, w_in, ssd_conv_w, ssd_conv_b,
                    ssd_dt_bias, ssd_a_log, ssd_d, ssd_norm_g, w_ssd_out, q_norm_g, k_norm_g,
                    idx_k_norm_g, w_att_out, w_o, norm2_g, w_up, ffn_conv_w, ffn_conv_b, w_down)
```

```python
import functools
import math
from typing import NamedTuple

import jax
import jax.numpy as jnp
from jax import lax
from jax.experimental import pallas as pl
from jax.experimental.pallas import tpu as pltpu

V7X_VMEM_BYTES = 64 * 1024 * 1024
LANE = 128
SUBLANE = 8
VMEM_BUDGET = V7X_VMEM_BYTES - 8 * 1024 * 1024
COMPILER_SCRATCH = 4 * 1024 * 1024

F32 = jnp.float32
BF16 = jnp.bfloat16
NEG_BIG = -1e30
LOG2_E = math.log2(math.e)
COUNT_UNROLL = 4
INT_MIN = -(2 ** 31)


class Cfg(NamedTuple):
    ssd_head_dim: int = 64
    ssd_groups: int = 8
    ssd_state: int = 128
    ssd_conv: int = 4
    ssd_chunk: int = 128
    ssd_expand: int = 2
    att_heads: int = 32
    att_head_dim: int = 128
    att_kv_groups: int = 4
    idx_heads: int = 32
    idx_dim: int = 128
    topk_max: int = 256
    q_block: int = 128
    ffn_mult: int = 2
    ffn_conv: int = 3
    n_mod: int = 6
    eps: float = 1e-6


def _round_up(x, m):
    return (x + m - 1) // m * m


def _params(sem, vmem_bytes):
    return pltpu.CompilerParams(dimension_semantics=sem,
                                vmem_limit_bytes=int(min(vmem_bytes + COMPILER_SCRATCH, VMEM_BUDGET)))


def _sigmoid(x):
    return 0.5 * jnp.tanh(0.5 * x) + 0.5


def _silu(x):
    return x * _sigmoid(x)


def _split3(x):
    hi = x.astype(BF16)
    rest = x - hi.astype(F32)
    mid = rest.astype(BF16)
    return hi, mid, (rest - mid.astype(F32)).astype(BF16)


def _rms(x, eps):
    return x * lax.rsqrt(jnp.mean(x * x, axis=-1, keepdims=True) + eps)


def _mm_body(*refs, nk, n_extra, epilogue):
    a_ref, b_ref = refs[0], refs[1]
    extra = refs[2:2 + n_extra]
    o_ref = refs[2 + n_extra]
    part = jnp.dot(a_ref[...], b_ref[...], preferred_element_type=F32)
    if nk == 1:
        o_ref[...] = epilogue(part, *[e[...] for e in extra]).astype(o_ref.dtype)
        return
    acc_ref = refs[3 + n_extra]
    k = pl.program_id(2)

    @pl.when(k == 0)
    def _():
        acc_ref[...] = part

    @pl.when(k > 0)
    def _():
        acc_ref[...] += part

    @pl.when(k == nk - 1)
    def _():
        o_ref[...] = epilogue(acc_ref[...], *[e[...] for e in extra]).astype(o_ref.dtype)


def _mm_blocks(m, k, n, a_bytes, b_bytes, out_bytes, n_tile_extra, row_group):
    for tm, tn, tk in ((1024, 1024, 4096), (1024, 1024, 2048), (1024, 512, 4096),
                       (512, 512, 4096), (512, 512, 2048), (1024, 128, 4096), (512, 256, 2048),
                       (256, 256, 2048), (256, 128, 1024), (128, 128, 512)):
        tk = min(tk, k)
        if m % tm or n % tn or k % tk or row_group % tm:
            continue
        windows = 2 * (tm * tk * a_bytes + tk * tn * b_bytes + tm * tn * out_bytes
                       + n_tile_extra * tm * tn * 4)
        temporaries = (3 if k > tk else 2) * tm * tn * 4
        need = windows + temporaries
        if need + COMPILER_SCRATCH <= VMEM_BUDGET:
            return tm, tn, tk, need
    raise ValueError(f"no matmul tiling for {(m, k, n)}")


def _identity(acc):
    return acc


def _matmul(a, b, *, out_dtype, epilogue=_identity, tiles=(), rows=(), name):
    m, k = a.shape
    n = b.shape[1]
    tm, tn, tk, need = _mm_blocks(m, k, n, a.dtype.itemsize, b.dtype.itemsize,
                                  jnp.dtype(out_dtype).itemsize, len(tiles),
                                  math.gcd(m, *[rpg for _, rpg in rows]))
    nk = k // tk
    in_specs = [pl.BlockSpec((tm, tk), lambda i, j, kk: (i, kk)),
                pl.BlockSpec((tk, tn), lambda i, j, kk: (kk, j))]
    for _, col0 in tiles:
        assert col0 % tn == 0
        in_specs.append(pl.BlockSpec((tm, tn), functools.partial(
            lambda i, j, kk, off: (i, j + off), off=col0 // tn)))
    for _, rows_per_group in rows:
        assert rows_per_group % tm == 0
        in_specs.append(pl.BlockSpec((None, 1, tn), functools.partial(
            lambda i, j, kk, tpg: (i // tpg, 0, j), tpg=rows_per_group // tm)))
    scratch = [pltpu.VMEM((tm, tn), F32)] if nk > 1 else []
    return pl.pallas_call(
        functools.partial(_mm_body, nk=nk, n_extra=len(tiles) + len(rows), epilogue=epilogue),
        grid=(m // tm, n // tn, nk),
        in_specs=in_specs,
        out_specs=pl.BlockSpec((tm, tn), lambda i, j, kk: (i, j)),
        out_shape=jax.ShapeDtypeStruct((m, n), out_dtype),
        scratch_shapes=scratch,
        compiler_params=_params(("parallel", "parallel", "arbitrary"), need),
        name=name,
    )(a, b, *[t for t, _ in tiles], *[r for r, _ in rows])


def _ada_body(c_ref, w_ref, b_ref, o_ref):
    o_ref[...] = jnp.dot(_silu(c_ref[...]), w_ref[...], preferred_element_type=F32) + b_ref[...]


def _ada_mod(c, w_ada, b_ada):
    bsz, d = c.shape
    n = w_ada.shape[1]
    mp = _round_up(bsz, SUBLANE)
    c_pad = jnp.zeros((mp, d), F32).at[:bsz].set(c)
    tn = 512 if n % 512 == 0 else LANE
    need = 2 * (mp * d * 4 + d * tn * 4 + tn * 4 + mp * tn * 4) + d * tn * 2
    out = pl.pallas_call(
        _ada_body,
        grid=(n // tn,),
        in_specs=[pl.BlockSpec((mp, d), lambda j: (0, 0)),
                  pl.BlockSpec((d, tn), lambda j: (0, j)),
                  pl.BlockSpec((1, tn), lambda j: (0, j))],
        out_specs=pl.BlockSpec((mp, tn), lambda j: (0, j)),
        out_shape=jax.ShapeDtypeStruct((mp, n), F32),
        compiler_params=_params(("parallel",), need),
        name="ada_mod",
    )(c_pad, w_ada, b_ada.reshape(1, n))
    return out[:bsz]


def _norm_mod_body(x_ref, g_ref, scale_ref, scale_t_ref, shift_ref, shift_t_ref, o_ref, *, eps):
    y = _rms(x_ref[...], eps) * g_ref[...]
    scale = scale_ref[...] + scale_t_ref[...]
    shift = shift_ref[...] + shift_t_ref[...]
    o_ref[...] = (y * (1.0 + scale) + shift).astype(o_ref.dtype)


def _norm_mod(x2, g, scale, scale_t, shift, shift_t, *, seq, eps, name):
    m, d = x2.shape
    tm = 256
    assert seq % tm == 0
    tiles_per_seq = seq // tm
    need = 2 * (tm * d * 4 + tm * d * 2 + 5 * d * 4) + 4 * tm * d * 4
    per_seq = pl.BlockSpec((None, 1, d), lambda i: (i // tiles_per_seq, 0, 0))
    shared = pl.BlockSpec((1, d), lambda i: (0, 0))
    return pl.pallas_call(
        functools.partial(_norm_mod_body, eps=eps),
        grid=(m // tm,),
        in_specs=[pl.BlockSpec((tm, d), lambda i: (i, 0)), shared, per_seq, shared, per_seq, shared],
        out_specs=pl.BlockSpec((tm, d), lambda i: (i, 0)),
        out_shape=jax.ShapeDtypeStruct((m, d), BF16),
        compiler_params=_params(("parallel",), need),
        name=name,
    )(x2, g.reshape(1, d), scale, scale_t.reshape(1, d), shift, shift_t.reshape(1, d))


def _shift_rows(x, prev8, s):
    rolled = pltpu.roll(x, s, 0)
    row = lax.broadcasted_iota(jnp.int32, (SUBLANE, x.shape[1]), 0)
    head = jnp.where(row < s, pltpu.roll(prev8, s, 0), rolled[:SUBLANE])
    return jnp.concatenate([head, rolled[SUBLANE:]], axis=0)


def _causal_conv(x, prev8, w, b):
    kw = w.shape[0]
    y = x * w[kw - 1:kw, :] + b
    for s in range(1, kw):
        y = y + _shift_rows(x, prev8, s) * w[kw - 1 - s:kw - s, :]
    return y


def _up_conv_body(h_ref, wa_ref, wb_ref, cwa_ref, cwb_ref, cba_ref, cbb_ref, o_ref,
                  prev_a, prev_b, *, tiles_per_seq):
    i = pl.program_id(1)

    @pl.when(i % tiles_per_seq == 0)
    def _():
        prev_a[...] = jnp.zeros_like(prev_a)
        prev_b[...] = jnp.zeros_like(prev_b)

    h = h_ref[...]
    ua = jnp.dot(h, wa_ref[...], preferred_element_type=F32)
    ub = jnp.dot(h, wb_ref[...], preferred_element_type=F32)
    a = _causal_conv(ua, prev_a[...], cwa_ref[...], cba_ref[...])
    b = _causal_conv(ub, prev_b[...], cwb_ref[...], cbb_ref[...])
    o_ref[...] = (_silu(a) * b).astype(o_ref.dtype)
    prev_a[...] = ua[-SUBLANE:]
    prev_b[...] = ub[-SUBLANE:]


def _up_conv_gate(h, w_up, conv_w, conv_b, *, seq, name):
    m, d = h.shape
    f = w_up.shape[1] // 2
    tm, tn = 512, 512
    assert seq % tm == 0 and f % tn == 0
    nf = f // tn
    kw = conv_w.shape[0]
    need = 2 * (tm * d * 2 + 2 * d * tn * 2 + tm * tn * 2) + 8 * tm * tn * 4
    cb = conv_b.reshape(1, 2 * f)
    return pl.pallas_call(
        functools.partial(_up_conv_body, tiles_per_seq=seq // tm),
        grid=(nf, m // tm),
        in_specs=[pl.BlockSpec((tm, d), lambda j, i: (i, 0)),
                  pl.BlockSpec((d, tn), lambda j, i: (0, j)),
                  pl.BlockSpec((d, tn), lambda j, i: (0, j + nf)),
                  pl.BlockSpec((kw, tn), lambda j, i: (0, j)),
                  pl.BlockSpec((kw, tn), lambda j, i: (0, j + nf)),
                  pl.BlockSpec((1, tn), lambda j, i: (0, j)),
                  pl.BlockSpec((1, tn), lambda j, i: (0, j + nf))],
        out_specs=pl.BlockSpec((tm, tn), lambda j, i: (i, j)),
        out_shape=jax.ShapeDtypeStruct((m, f), BF16),
        scratch_shapes=[pltpu.VMEM((SUBLANE, tn), F32), pltpu.VMEM((SUBLANE, tn), F32)],
        compiler_params=_params(("parallel", "arbitrary"), need),
        name=name,
    )(h, w_up, w_up, conv_w, conv_w, cb, cb)


_NT = (((1,), (1,)), ((), ()))


def _softplus(x):
    return jnp.maximum(x, 0.0) + jnp.log1p(jnp.exp(-jnp.abs(x)))


def _conv_from_history(ext_ref, w, b):
    t = ext_ref.shape[0] - SUBLANE
    kw = w.shape[0]
    y = ext_ref[SUBLANE:, :] * w[kw - 1:kw, :] + b
    for s in range(1, kw):
        y = y + ext_ref[SUBLANE - s:SUBLANE - s + t, :] * w[kw - 1 - s:kw - s, :]
    return y


def _ssd_body(z_ref, xs_ref, b_ref, c_ref, dt_ref, hpar_ref, cwx_ref, cwb_ref, cwc_ref,
              cbx_ref, cbb_ref, cbc_ref, ng_ref, o_ref, state_ref, ex_ref, eb_ref, ec_ref,
              *, hg, hd, eps):
    g = pl.program_id(1)
    q, hp = xs_ref.shape
    hall = dt_ref.shape[1]
    assert 2 * hd == LANE and hp == hg * hd and hg % 2 == 0

    @pl.when(pl.program_id(2) == 0)
    def _():
        state_ref[...] = jnp.zeros_like(state_ref)
        for ext in (ex_ref, eb_ref, ec_ref):
            ext[:SUBLANE, :] = jnp.zeros((SUBLANE, ext.shape[1]), F32)

    ex_ref[SUBLANE:, :] = xs_ref[...]
    eb_ref[SUBLANE:, :] = b_ref[...]
    ec_ref[SUBLANE:, :] = c_ref[...]
    xs = _silu(_conv_from_history(ex_ref, cwx_ref[...], cbx_ref[...]))
    bm = _silu(_conv_from_history(eb_ref, cwb_ref[...], cbb_ref[...]))
    cm = _silu(_conv_from_history(ec_ref, cwc_ref[...], cbc_ref[...]))
    for ext in (ex_ref, eb_ref, ec_ref):
        ext[:SUBLANE, :] = ext[q:, :]

    hpar = hpar_ref[...]
    dt_all = _softplus(dt_ref[...] + hpar[0:1])
    da_all = dt_all * (-jnp.exp(hpar[1:2]))
    row = lax.broadcasted_iota(jnp.int32, (q, q), 0)
    col = lax.broadcasted_iota(jnp.int32, (q, q), 1)
    causal = col <= row
    ac3 = jnp.dot(causal.astype(BF16), jnp.concatenate(_split3(da_all), axis=1),
                  preferred_element_type=F32)
    acum_all = ac3[:, :hall] + ac3[:, hall:2 * hall] + ac3[:, 2 * hall:]

    head_of_chan = g * hg + lax.broadcasted_iota(jnp.int32, (hall, hp), 1) // hd
    expand = (lax.broadcasted_iota(jnp.int32, (hall, hp), 0) == head_of_chan).astype(BF16)
    pick = (lax.broadcasted_iota(jnp.int32, (hg, hall), 1)
            == g * hg + lax.broadcasted_iota(jnp.int32, (hg, hall), 0)).astype(BF16)
    stack = jnp.concatenate([dt_all, acum_all, hpar, hpar], axis=0)
    ns = stack.shape[0]
    sp3 = jnp.dot(jnp.concatenate(_split3(stack), axis=0), expand, preferred_element_type=F32)
    spread = sp3[:ns] + sp3[ns:2 * ns] + sp3[2 * ns:]
    dt_x, acum_x, d_x = spread[:q], spread[q:2 * q], spread[2 * q + 2:2 * q + 3]
    at3 = lax.dot_general(pick, jnp.concatenate(_split3(acum_all), axis=0), _NT,
                          preferred_element_type=F32)
    acum_t = at3[:, :q] + at3[:, q:2 * q] + at3[:, 2 * q:]
    last_x = acum_x[q - 1:q]

    xdt = xs * dt_x
    xdt_b = xdt.astype(BF16)
    cm_b = cm.astype(BF16)
    cb = lax.dot_general(cm_b, bm.astype(BF16), _NT, preferred_element_type=F32)

    lane_head = lax.broadcasted_iota(jnp.int32, (q, hall), 1)
    pair_rows = lax.broadcasted_iota(jnp.int32, (2 * q, LANE), 0) < q
    pair_lanes = lax.broadcasted_iota(jnp.int32, (2 * q, LANE), 1) < hd
    y_parts = []
    for pair in range(hg // 2):
        ms = []
        for h in (2 * pair, 2 * pair + 1):
            a_col = jnp.sum(jnp.where(lane_head == g * hg + h, acum_all, 0.0), axis=1,
                            keepdims=True)
            seg = a_col - acum_t[h:h + 1, :]
            ms.append((cb * jnp.exp(jnp.where(causal, seg, -jnp.inf))).astype(BF16))
        xk = xdt_b[:, pair * LANE:(pair + 1) * LANE]
        rhs = jnp.where(pair_rows == pair_lanes, jnp.concatenate([xk, xk], axis=0),
                        jnp.zeros((), BF16))
        y_parts.append(jnp.dot(jnp.concatenate(ms, axis=1), rhs, preferred_element_type=F32))
    y = jnp.concatenate(y_parts, axis=1)

    state = state_ref[...]
    y = y + jnp.dot(cm_b, state.astype(BF16), preferred_element_type=F32) * jnp.exp(acum_x)
    xw = (xdt * jnp.exp(last_x - acum_x)).astype(BF16)
    state_ref[...] = state * jnp.exp(last_x) + jnp.dot(
        bm.T.astype(BF16), xw, preferred_element_type=F32)

    y = y + d_x * xs
    yg = y * _silu(z_ref[...])
    yg = yg * lax.rsqrt(jnp.mean(yg * yg, axis=-1, keepdims=True) + eps)
    o_ref[...] = (yg * ng_ref[...]).astype(o_ref.dtype)


def _ssd_branch(p_ssd, dt_raw, hpar, conv_w, conv_b, norm_g, *, cfg, name):
    bsz, seq, _ = p_ssd.shape
    g_n, n, hd, q = cfg.ssd_groups, cfg.ssd_state, cfg.ssd_head_dim, cfg.ssd_chunk
    d_inner = norm_g.shape[0]
    hp = d_inner // g_n
    hg = hp // hd
    hall = hpar.shape[1]
    kw = conv_w.shape[0]
    xs0 = d_inner // hp
    b0 = 2 * d_inner // n
    c0 = b0 + g_n
    cwx, cwb, cwc = (conv_w[:, :d_inner], conv_w[:, d_inner:d_inner + g_n * n],
                     conv_w[:, d_inner + g_n * n:])
    cb2 = conv_b.reshape(1, -1)
    cbx, cbb, cbc = cb2[:, :d_inner], cb2[:, d_inner:d_inner + g_n * n], cb2[:, d_inner + g_n * n:]
    need = 2 * (2 * q * hp * 4 + 2 * q * n * 4 + q * hall * 4 + q * hp * 2) + 24 * q * hp * 4
    return pl.pallas_call(
        functools.partial(_ssd_body, hg=hg, hd=hd, eps=cfg.eps),
        grid=(bsz, g_n, seq // q),
        in_specs=[
            pl.BlockSpec((None, q, hp), lambda b, g, c: (b, c, g)),
            pl.BlockSpec((None, q, hp), lambda b, g, c: (b, c, xs0 + g)),
            pl.BlockSpec((None, q, n), lambda b, g, c: (b, c, b0 + g)),
            pl.BlockSpec((None, q, n), lambda b, g, c: (b, c, c0 + g)),
            pl.BlockSpec((None, q, hall), lambda b, g, c: (b, c, 0)),
            pl.BlockSpec((SUBLANE, hall), lambda b, g, c: (0, 0)),
            pl.BlockSpec((kw, hp), lambda b, g, c: (0, g)),
            pl.BlockSpec((kw, n), lambda b, g, c: (0, g)),
            pl.BlockSpec((kw, n), lambda b, g, c: (0, g)),
            pl.BlockSpec((1, hp), lambda b, g, c: (0, g)),
            pl.BlockSpec((1, n), lambda b, g, c: (0, g)),
            pl.BlockSpec((1, n), lambda b, g, c: (0, g)),
            pl.BlockSpec((1, hp), lambda b, g, c: (0, g)),
        ],
        out_specs=pl.BlockSpec((None, q, hp), lambda b, g, c: (b, c, g)),
        out_shape=jax.ShapeDtypeStruct((bsz, seq, d_inner), BF16),
        scratch_shapes=[pltpu.VMEM((n, hp), F32), pltpu.VMEM((SUBLANE + q, hp), F32),
                        pltpu.VMEM((SUBLANE + q, n), F32), pltpu.VMEM((SUBLANE + q, n), F32)],
        compiler_params=_params(("parallel", "parallel", "arbitrary"), need),
        name=name,
    )(p_ssd, p_ssd, p_ssd, p_ssd, dt_raw, hpar, cwx, cwb, cwc, cbx, cbb, cbc,
      norm_g.reshape(1, d_inner))


def _dsa_prep_body(k_ref, v_ref, ki_ref, kg_ref, kig_ref, kn_ref, vt_ref, kin_ref, *, hd, eps):
    for g in range(k_ref.shape[1] // hd):
        sl = slice(g * hd, (g + 1) * hd)
        kn_ref[:, sl] = (_rms(k_ref[:, sl], eps) * kg_ref[...]).astype(kn_ref.dtype)
        vt_ref[sl, :] = v_ref[:, sl].T.astype(vt_ref.dtype)
    kin_ref[...] = (_rms(ki_ref[...], eps) * kig_ref[...]).astype(kin_ref.dtype)


def _dsa_prep(p_att, p_small, k_norm_g, ki_norm_g, *, cfg, k_blk, v_blk, ki_blk, name):
    bsz, seq, _ = p_att.shape
    hd, idim, tk = cfg.att_head_dim, cfg.idx_dim, cfg.q_block
    kvw = cfg.att_kv_groups * hd
    nb = seq // tk
    need = 2 * (2 * tk * kvw * 4 + tk * idim * 4 + 2 * tk * kvw * 2 + tk * idim * 2) + 4 * tk * kvw * 4
    return pl.pallas_call(
        functools.partial(_dsa_prep_body, hd=hd, eps=cfg.eps),
        grid=(bsz, nb),
        in_specs=[pl.BlockSpec((None, tk, kvw), lambda b, i: (b, i, k_blk)),
                  pl.BlockSpec((None, tk, kvw), lambda b, i: (b, i, v_blk)),
                  pl.BlockSpec((None, tk, idim), lambda b, i: (b, i, ki_blk)),
                  pl.BlockSpec((1, hd), lambda b, i: (0, 0)),
                  pl.BlockSpec((1, idim), lambda b, i: (0, 0))],
        out_specs=[pl.BlockSpec((None, tk, kvw), lambda b, i: (b, i, 0)),
                   pl.BlockSpec((None, None, kvw, tk), lambda b, i: (b, i, 0, 0)),
                   pl.BlockSpec((None, tk, idim), lambda b, i: (b, i, 0))],
        out_shape=[jax.ShapeDtypeStruct((bsz, seq, kvw), BF16),
                   jax.ShapeDtypeStruct((bsz, nb, kvw, tk), BF16),
                   jax.ShapeDtypeStruct((bsz, seq, idim), BF16)],
        compiler_params=_params(("parallel", "parallel"), need),
        name=name,
    )(p_att, p_att, p_small, k_norm_g.reshape(1, hd), ki_norm_g.reshape(1, idim))


def _dsa_body(q_ref, qi_ref, wi_ref, kn_ref, vt_ref, kin_ref, qg_ref, o_ref,
              qn_s, qib_s, key_s, m_s, l_s, acc_s, sa_s, sb_s, ba_s, bb_s,
              *, nh, hd, kvg, ih, idim, ksel, eps):
    i = pl.program_id(1)
    tq = q_ref.shape[0]
    tk = tq
    hpg = nh // kvg
    hq = hpg * tq
    assert ih % hpg == 0

    for h in range(nh):
        qh = _rms(q_ref[:, h * hd:(h + 1) * hd], eps) * qg_ref[...] * (hd ** -0.5 * LOG2_E)
        qn_s[:, h * tq:(h + 1) * tq] = qh.T.astype(BF16)
    for h in range(ih):
        qib_s[:, h * tq:(h + 1) * tq] = qi_ref[:, h * idim:(h + 1) * idim].T.astype(BF16)
    wi_t = wi_ref[...].T * (idim ** -0.5 * ih ** -0.5)

    kpos = lax.broadcasted_iota(jnp.int32, (tk, tq), 0)
    qpos = lax.broadcasted_iota(jnp.int32, (tk, tq), 1)
    future = kpos > qpos

    def score_blk(kb, carry):
        ki_blk = kin_ref[pl.ds(pl.multiple_of(kb * tk, tk), tk), :]
        acc = jnp.zeros((tk, tq), F32)
        for h0 in range(0, ih, hpg):
            lt = jnp.dot(ki_blk, qib_s[:, h0 * tq:(h0 + hpg) * tq],
                         preferred_element_type=F32)
            for h in range(h0, h0 + hpg):
                acc = acc + wi_t[h:h + 1, :] * jnp.maximum(lt[:, (h - h0) * tq:(h - h0 + 1) * tq], 0.0)
        bits = pltpu.bitcast(acc, jnp.int32)
        key_s[kb] = bits ^ ((bits >> 31) & jnp.int32(0x7FFFFFFF))
        return carry

    lax.fori_loop(0, i + 1, score_blk, 0)
    key_s[i] = jnp.where(future, jnp.int32(INT_MIN), key_s[i])
    for u in range(1, COUNT_UNROLL):
        key_s[i + u] = jnp.full((tk, tq), INT_MIN, jnp.int32)
    n_steps = (i + COUNT_UNROLL) // COUNT_UNROLL

    def count_keys(accept):
        def step(j, cnt):
            for u in range(COUNT_UNROLL):
                cnt = cnt + jnp.where(accept(key_s[j * COUNT_UNROLL + u]), 1.0, 0.0)
            return cnt
        cnt = lax.fori_loop(0, n_steps, step, jnp.zeros((tk, tq), F32))
        return jnp.sum(cnt, axis=0, keepdims=True)

    def bit_step(bi, thr):
        cand = thr + (jnp.int32(1) << (31 - bi))
        return jnp.where(count_keys(lambda k: k >= cand) >= ksel, cand, thr)

    thr = lax.fori_loop(0, 32, bit_step, jnp.full((1, tq), INT_MIN, jnp.int32))

    ties_wanted = ksel - count_keys(lambda k: k > thr)
    earlier_key = (lax.broadcasted_iota(jnp.int32, (tk, tk), 1)
                   < lax.broadcasted_iota(jnp.int32, (tk, tk), 0)).astype(BF16)

    m_s[...] = jnp.full_like(m_s, NEG_BIG)
    l_s[...] = jnp.zeros_like(l_s)
    acc_s[...] = jnp.zeros_like(acc_s)

    def logits_into(s_ref, kb, g):
        row0 = pl.multiple_of(kb * tk, tk)
        s_ref[g] = jnp.dot(kn_ref[pl.ds(row0, tk), g * hd:(g + 1) * hd],
                           qn_s[:, g * hq:(g + 1) * hq], preferred_element_type=F32)

    def mask_into(bias_ref, kb, ties_before):
        keys = key_s[kb]
        tie = jnp.where(keys == thr, 1.0, 0.0)
        tie_rank = ties_before + jnp.dot(earlier_key, tie.astype(BF16),
                                         preferred_element_type=F32)
        keep_tie = jnp.where(tie_rank < ties_wanted, tie, 0.0)
        keep = jnp.where(keys > thr, 1.0, keep_tie)
        on_diag = (kb == i).astype(F32)
        keep = keep - jnp.where(future, on_diag, 0.0)
        bias_ref[...] = jnp.where(keep > 0.5, 0.0, NEG_BIG)
        return ties_before + jnp.sum(tie, axis=0, keepdims=True)

    def att_step(kb, ties_before, cur, nxt):
        (s_cur, bias_cur), (s_next, bias_next) = cur, nxt
        kb_next = jnp.minimum(kb + 1, i)
        bias = jnp.concatenate([bias_cur[...]] * hpg, axis=1)
        for g in range(kvg):
            logits_into(s_next, kb_next, g)
            vt_blk = vt_ref[kb, g * hd:(g + 1) * hd, :]
            s = s_cur[g] + bias
            m_old = m_s[g:g + 1, :]
            m_new = jnp.maximum(m_old, jnp.max(s, axis=0, keepdims=True))
            p = jnp.exp2(s - m_new)
            alpha = jnp.exp2(m_old - m_new)
            l_s[g:g + 1, :] = alpha * l_s[g:g + 1, :] + jnp.sum(p, axis=0, keepdims=True)
            acc_s[g * hd:(g + 1) * hd, :] = alpha * acc_s[g * hd:(g + 1) * hd, :] + jnp.dot(
                vt_blk, p.astype(BF16), preferred_element_type=F32)
            m_s[g:g + 1, :] = m_new
        return mask_into(bias_next, kb_next, ties_before)

    def att_blk(kb, ties_before):
        buf_a, buf_b = (sa_s, ba_s), (sb_s, bb_s)
        return lax.cond(kb % 2 == 0,
                        lambda t: att_step(kb, t, buf_a, buf_b),
                        lambda t: att_step(kb, t, buf_b, buf_a), ties_before)

    for g in range(kvg):
        logits_into(sa_s, 0, g)
    ties_before_1 = mask_into(ba_s, 0, jnp.zeros((1, tq), F32))
    lax.fori_loop(0, i + 1, att_blk, ties_before_1)

    for g in range(kvg):
        o_t = acc_s[g * hd:(g + 1) * hd, :] / l_s[g:g + 1, :]
        for hh in range(hpg):
            h = g * hpg + hh
            o_ref[:, h * hd:(h + 1) * hd] = o_t[:, hh * tq:(hh + 1) * tq].T.astype(o_ref.dtype)


def _dsa_attend(p_att, p_small, kn, vt, kin, q_norm_g, *, cfg, wi_blk, name):
    bsz, seq, _ = p_att.shape
    nh, hd, kvg = cfg.att_heads, cfg.att_head_dim, cfg.att_kv_groups
    ih, idim, tq = cfg.idx_heads, cfg.idx_dim, cfg.q_block
    assert nh * hd == ih * idim and ih <= LANE
    aw, kvw, nb, hpg = nh * hd, kvg * hd, seq // tq, nh // kvg
    ksel = min(cfg.topk_max, seq // 4)
    need = (2 * (2 * tq * aw * 4 + tq * LANE * 4 + 2 * seq * kvw * 2 + seq * idim * 2 + tq * aw * 2)
            + 2 * nh * tq * hd * 2 + seq * tq * 4 + 3 * nh * hd * tq * 4 + 16 * tq * tq * 4)
    return pl.pallas_call(
        functools.partial(_dsa_body, nh=nh, hd=hd, kvg=kvg, ih=ih, idim=idim, ksel=float(ksel),
                          eps=cfg.eps),
        grid=(bsz, nb),
        in_specs=[pl.BlockSpec((None, tq, aw), lambda b, i: (b, i, 0)),
                  pl.BlockSpec((None, tq, aw), lambda b, i: (b, i, 1)),
                  pl.BlockSpec((None, tq, LANE), lambda b, i: (b, i, wi_blk)),
                  pl.BlockSpec((None, seq, kvw), lambda b, i: (b, 0, 0)),
                  pl.BlockSpec((None, nb, kvw, tq), lambda b, i: (b, 0, 0, 0)),
                  pl.BlockSpec((None, seq, idim), lambda b, i: (b, 0, 0)),
                  pl.BlockSpec((1, hd), lambda b, i: (0, 0))],
        out_specs=pl.BlockSpec((None, tq, aw), lambda b, i: (b, i, 0)),
        out_shape=jax.ShapeDtypeStruct((bsz, seq, aw), BF16),
        scratch_shapes=[pltpu.VMEM((hd, nh * tq), BF16), pltpu.VMEM((idim, ih * tq), BF16),
                        pltpu.VMEM((nb + COUNT_UNROLL - 1, tq, tq), jnp.int32),
                        pltpu.VMEM((kvg, hpg * tq), F32), pltpu.VMEM((kvg, hpg * tq), F32),
                        pltpu.VMEM((kvg * hd, hpg * tq), F32),
                        pltpu.VMEM((kvg, tq, hpg * tq), F32), pltpu.VMEM((kvg, tq, hpg * tq), F32),
                        pltpu.VMEM((tq, tq), F32), pltpu.VMEM((tq, tq), F32)],
        compiler_params=_params(("parallel", "arbitrary"), need),
        name=name,
    )(p_att, p_att, p_small, kn, vt, kin, q_norm_g.reshape(1, hd))


def _pad_cols(w, width):
    return jnp.pad(w, ((0, 0), (0, width - w.shape[1])))


def _pack_w_in(w, cfg, d_model):
    d_inner = cfg.ssd_expand * d_model
    conv_ch = d_inner + 2 * cfg.ssd_groups * cfg.ssd_state
    heads = d_inner // cfg.ssd_head_dim
    aw = cfg.att_heads * cfg.att_head_dim
    kvw = cfg.att_kv_groups * cfg.att_head_dim
    sizes = (d_inner, conv_ch, heads, aw, kvw, kvw, cfg.idx_heads * cfg.idx_dim, cfg.idx_dim,
             cfg.idx_heads, d_model, d_model)
    assert sum(sizes) == w.shape[1]
    parts, o = [], 0
    for s in sizes:
        parts.append(w[:, o:o + s])
        o += s
    z, xbc, dt, q, k, v, qi, ki, wi, g_ssd, g_att = parts
    hall = _round_up(heads, LANE)
    cat = lambda ws: jnp.concatenate(ws, axis=1).astype(BF16)
    return (cat([z, xbc]), cat([q, qi, k, v]), cat([g_ssd, g_att]),
            cat([_pad_cols(dt, hall), ki, _pad_cols(wi, LANE)]), hall)


def _gated(acc, g):
    return _sigmoid(g) * acc


def _gated_add(acc, g, prev):
    return prev + _sigmoid(g) * acc


def _residual(acc, x, gate, gate_t):
    return x + (gate + gate_t) * acc


def _forward(cfg, x, c, w_ada, b_ada, ada_table, norm1_g, w_in, ssd_conv_w, ssd_conv_b,
             ssd_dt_bias, ssd_a_log, ssd_d, ssd_norm_g, w_ssd_out, q_norm_g, k_norm_g,
             idx_k_norm_g, w_att_out, w_o, norm2_g, w_up, ffn_conv_w, ffn_conv_b, w_down):
    bsz, seq, d = x.shape
    m = bsz * seq
    depth = w_in.shape[0]
    aw = cfg.att_heads * cfg.att_head_dim
    kvw = cfg.att_kv_groups * cfg.att_head_dim
    heads = cfg.ssd_expand * d // cfg.ssd_head_dim

    mod = _ada_mod(c, w_ada, b_ada).reshape(bsz, cfg.n_mod, d)
    shift1, scale1, gate1, shift2, scale2, gate2 = [mod[:, i:i + 1, :] for i in range(cfg.n_mod)]
    x2 = x.reshape(m, d)
    for l in range(depth):
        tab = ada_table[l]
        w_ssd, w_att, w_gate, w_small, hall = _pack_w_in(w_in[l], cfg, d)
        hpar = jnp.zeros((SUBLANE, hall), F32)
        hpar = hpar.at[0, :heads].set(ssd_dt_bias[l]).at[1, :heads].set(ssd_a_log[l])
        hpar = hpar.at[2, :heads].set(ssd_d[l])

        h = _norm_mod(x2, norm1_g[l], scale1, tab[1], shift1, tab[0], seq=seq, eps=cfg.eps,
                      name=f"norm1_l{l}")
        p_ssd = _matmul(h, w_ssd, out_dtype=F32, name=f"in_ssd_l{l}").reshape(bsz, seq, -1)
        p_att = _matmul(h, w_att, out_dtype=F32, name=f"in_att_l{l}").reshape(bsz, seq, -1)
        p_gate = _matmul(h, w_gate, out_dtype=F32, name=f"in_gate_l{l}")
        p_small = _matmul(h, w_small, out_dtype=F32, name=f"in_small_l{l}").reshape(bsz, seq, -1)

        y_ssd = _ssd_branch(p_ssd, p_small, hpar, ssd_conv_w[l], ssd_conv_b[l], ssd_norm_g[l],
                            cfg=cfg, name=f"ssd_l{l}")
        kn, vt, kin = _dsa_prep(p_att, p_small, k_norm_g[l], idx_k_norm_g[l], cfg=cfg,
                                k_blk=2 * aw // kvw, v_blk=2 * aw // kvw + 1,
                                ki_blk=hall // cfg.idx_dim, name=f"dsa_prep_l{l}")
        o_att = _dsa_attend(p_att, p_small, kn, vt, kin, q_norm_g[l], cfg=cfg,
                            wi_blk=(hall + cfg.idx_dim) // LANE, name=f"dsa_l{l}")

        part = _matmul(y_ssd.reshape(m, -1), w_ssd_out[l].astype(BF16), out_dtype=F32,
                       epilogue=_gated, tiles=[(p_gate, 0)], name=f"ssd_out_l{l}")
        merged = _matmul(o_att.reshape(m, -1), w_att_out[l].astype(BF16), out_dtype=BF16,
                         epilogue=_gated_add, tiles=[(p_gate, d), (part, 0)],
                         name=f"att_out_l{l}")
        x2 = _matmul(merged, w_o[l].astype(BF16), out_dtype=F32, epilogue=_residual,
                     tiles=[(x2, 0)], rows=[(gate1, seq), (tab[2].reshape(1, 1, d), m)],
                     name=f"mix_out_l{l}")

        h = _norm_mod(x2, norm2_g[l], scale2, tab[4], shift2, tab[3], seq=seq, eps=cfg.eps,
                      name=f"norm2_l{l}")
        act = _up_conv_gate(h, w_up[l].astype(BF16), ffn_conv_w[l], ffn_conv_b[l], seq=seq,
                            name=f"ffn_up_l{l}")
        x2 = _matmul(act, w_down[l].astype(BF16), out_dtype=F32, epilogue=_residual,
                     tiles=[(x2, 0)], rows=[(gate2, seq), (tab[5].reshape(1, 1, d), m)],
                     name=f"ffn_down_l{l}")
    return x2.reshape(bsz, seq, d)


def kernel(x, c, w_ada, b_ada, ada_table, norm1_g, w_in, ssd_conv_w, ssd_conv_b, ssd_dt_bias,
           ssd_a_log, ssd_d, ssd_norm_g, w_ssd_out, q_norm_g, k_norm_g, idx_k_norm_g, w_att_out,
           w_o, norm2_g, w_up, ffn_conv_w, ffn_conv_b, w_down):
    return _forward(Cfg(), x, c, w_ada, b_ada, ada_table, norm1_g, w_in, ssd_conv_w, ssd_conv_b,
                    ssd_dt_bias, ssd_a_log, ssd_d, ssd_norm_g, w_ssd_out, q_norm_g, k_norm_g,
                    idx_k_norm_g, w_att_out, w_o, norm2_g, w_up, ffn_conv_w, ffn_conv_b, w_down)
```

```python
import functools
import math
from typing import NamedTuple

import jax
import jax.numpy as jnp
from jax import lax
from jax.experimental import pallas as pl
from jax.experimental.pallas import tpu as pltpu

V7X_VMEM_BYTES = 64 * 1024 * 1024
LANE = 128
SUBLANE = 8
VMEM_BUDGET = V7X_VMEM_BYTES - 8 * 1024 * 1024
COMPILER_SCRATCH = 4 * 1024 * 1024

F32 = jnp.float32
BF16 = jnp.bfloat16
NEG_BIG = -1e30
LOG2_E = math.log2(math.e)
COUNT_UNROLL = 4
INT_MIN = -(2 ** 31)


class Cfg(NamedTuple):
    ssd_head_dim: int = 64
    ssd_groups: int = 8
    ssd_state: int = 128
    ssd_conv: int = 4
    ssd_chunk: int = 128
    ssd_expand: int = 2
    att_heads: int = 32
    att_head_dim: int = 128
    att_kv_groups: int = 4
    idx_heads: int = 32
    idx_dim: int = 128
    topk_max: int = 256
    q_block: int = 128
    ffn_mult: int = 2
    ffn_conv: int = 3
    n_mod: int = 6
    eps: float = 1e-6


def _round_up(x, m):
    return (x + m - 1) // m * m


def _params(sem, vmem_bytes):
    return pltpu.CompilerParams(dimension_semantics=sem,
                                vmem_limit_bytes=int(min(vmem_bytes + COMPILER_SCRATCH, VMEM_BUDGET)))


def _sigmoid(x):
    return 0.5 * jnp.tanh(0.5 * x) + 0.5


def _silu(x):
    return x * _sigmoid(x)


def _split3(x):
    hi = x.astype(BF16)
    rest = x - hi.astype(F32)
    mid = rest.astype(BF16)
    return hi, mid, (rest - mid.astype(F32)).astype(BF16)


def _rms(x, eps):
    return x * lax.rsqrt(jnp.mean(x * x, axis=-1, keepdims=True) + eps)


def _mm_body(*refs, nk, n_extra, epilogue):
    a_ref, b_ref = refs[0], refs[1]
    extra = refs[2:2 + n_extra]
    o_ref = refs[2 + n_extra]
    part = jnp.dot(a_ref[...], b_ref[...], preferred_element_type=F32)
    if nk == 1:
        o_ref[...] = epilogue(part, *[e[...] for e in extra]).astype(o_ref.dtype)
        return
    acc_ref = refs[3 + n_extra]
    k = pl.program_id(2)

    @pl.when(k == 0)
    def _():
        acc_ref[...] = part

    @pl.when(k > 0)
    def _():
        acc_ref[...] += part

    @pl.when(k == nk - 1)
    def _():
        o_ref[...] = epilogue(acc_ref[...], *[e[...] for e in extra]).astype(o_ref.dtype)


def _mm_blocks(m, k, n, a_bytes, b_bytes, out_bytes, n_tile_extra, row_group):
    for tm, tn, tk in ((1024, 1024, 4096), (1024, 512, 4096), (1024, 1024, 2048),
                       (512, 512, 4096), (512, 512, 2048), (1024, 128, 4096), (512, 256, 2048),
                       (256, 256, 2048), (256, 128, 1024), (128, 128, 512)):
        tk = min(tk, k)
        if m % tm or n % tn or k % tk or row_group % tm:
            continue
        windows = 2 * (tm * tk * a_bytes + tk * tn * b_bytes + tm * tn * out_bytes
                       + n_tile_extra * tm * tn * 4)
        temporaries = (3 if k > tk else 2) * tm * tn * 4
        need = windows + temporaries
        if need + COMPILER_SCRATCH <= VMEM_BUDGET:
            return tm, tn, tk, need
    raise ValueError(f"no matmul tiling for {(m, k, n)}")


def _identity(acc):
    return acc


def _matmul(a, b, *, out_dtype, epilogue=_identity, tiles=(), rows=(), name):
    m, k = a.shape
    n = b.shape[1]
    tm, tn, tk, need = _mm_blocks(m, k, n, a.dtype.itemsize, b.dtype.itemsize,
                                  jnp.dtype(out_dtype).itemsize, len(tiles),
                                  math.gcd(m, *[rpg for _, rpg in rows]))
    nk = k // tk
    in_specs = [pl.BlockSpec((tm, tk), lambda i, j, kk: (i, kk)),
                pl.BlockSpec((tk, tn), lambda i, j, kk: (kk, j))]
    for _, col0 in tiles:
        assert col0 % tn == 0
        in_specs.append(pl.BlockSpec((tm, tn), functools.partial(
            lambda i, j, kk, off: (i, j + off), off=col0 // tn)))
    for _, rows_per_group in rows:
        assert rows_per_group % tm == 0
        in_specs.append(pl.BlockSpec((None, 1, tn), functools.partial(
            lambda i, j, kk, tpg: (i // tpg, 0, j), tpg=rows_per_group // tm)))
    scratch = [pltpu.VMEM((tm, tn), F32)] if nk > 1 else []
    return pl.pallas_call(
        functools.partial(_mm_body, nk=nk, n_extra=len(tiles) + len(rows), epilogue=epilogue),
        grid=(m // tm, n // tn, nk),
        in_specs=in_specs,
        out_specs=pl.BlockSpec((tm, tn), lambda i, j, kk: (i, j)),
        out_shape=jax.ShapeDtypeStruct((m, n), out_dtype),
        scratch_shapes=scratch,
        compiler_params=_params(("parallel", "parallel", "arbitrary"), need),
        name=name,
    )(a, b, *[t for t, _ in tiles], *[r for r, _ in rows])


def _ada_body(c_ref, w_ref, b_ref, o_ref):
    part = jnp.dot(_silu(c_ref[...]), w_ref[...], preferred_element_type=F32)

    @pl.when(pl.program_id(0) == 0)
    def _():
        o_ref[...] = part + b_ref[...]

    @pl.when(pl.program_id(0) > 0)
    def _():
        o_ref[...] += part


def _ada_mod(c, w_ada, b_ada):
    bsz, d = c.shape
    n = w_ada.shape[1]
    mp = _round_up(bsz, SUBLANE)
    c_pad = jnp.zeros((mp, d), F32).at[:bsz].set(c)
    tk = d
    while tk % (2 * LANE) == 0 and tk * n * 4 > 12 * 2 ** 20:
        tk //= 2
    need = 2 * (mp * tk * 4 + tk * n * 4 + n * 4 + mp * n * 4) + 2 * mp * n * 4
    out = pl.pallas_call(
        _ada_body,
        grid=(d // tk,),
        in_specs=[pl.BlockSpec((mp, tk), lambda k: (0, k)),
                  pl.BlockSpec((tk, n), lambda k: (k, 0)),
                  pl.BlockSpec((1, n), lambda k: (0, 0))],
        out_specs=pl.BlockSpec((mp, n), lambda k: (0, 0)),
        out_shape=jax.ShapeDtypeStruct((mp, n), F32),
        compiler_params=_params(("arbitrary",), need),
        name="ada_mod",
    )(c_pad, w_ada, b_ada.reshape(1, n))
    return out[:bsz]


def _norm_mod_body(x_ref, g_ref, scale_ref, scale_t_ref, shift_ref, shift_t_ref, o_ref, *, eps):
    y = _rms(x_ref[...], eps) * g_ref[...]
    scale = scale_ref[...] + scale_t_ref[...]
    shift = shift_ref[...] + shift_t_ref[...]
    o_ref[...] = (y * (1.0 + scale) + shift).astype(o_ref.dtype)


def _norm_mod(x2, g, scale, scale_t, shift, shift_t, *, seq, eps, name):
    m, d = x2.shape
    tm = 256
    assert seq % tm == 0
    tiles_per_seq = seq // tm
    need = 2 * (tm * d * 4 + tm * d * 2 + 5 * d * 4) + 4 * tm * d * 4
    per_seq = pl.BlockSpec((None, 1, d), lambda i: (i // tiles_per_seq, 0, 0))
    shared = pl.BlockSpec((1, d), lambda i: (0, 0))
    return pl.pallas_call(
        functools.partial(_norm_mod_body, eps=eps),
        grid=(m // tm,),
        in_specs=[pl.BlockSpec((tm, d), lambda i: (i, 0)), shared, per_seq, shared, per_seq, shared],
        out_specs=pl.BlockSpec((tm, d), lambda i: (i, 0)),
        out_shape=jax.ShapeDtypeStruct((m, d), BF16),
        compiler_params=_params(("parallel",), need),
        name=name,
    )(x2, g.reshape(1, d), scale, scale_t.reshape(1, d), shift, shift_t.reshape(1, d))


def _shift_rows(x, prev8, s):
    rolled = pltpu.roll(x, s, 0)
    row = lax.broadcasted_iota(jnp.int32, (SUBLANE, x.shape[1]), 0)
    head = jnp.where(row < s, pltpu.roll(prev8, s, 0), rolled[:SUBLANE])
    return jnp.concatenate([head, rolled[SUBLANE:]], axis=0)


def _causal_conv(x, prev8, w, b):
    kw = w.shape[0]
    y = x * w[kw - 1:kw, :] + b
    for s in range(1, kw):
        y = y + _shift_rows(x, prev8, s) * w[kw - 1 - s:kw - s, :]
    return y


def _up_conv_body(h_ref, wa_ref, wb_ref, cwa_ref, cwb_ref, cba_ref, cbb_ref, o_ref,
                  prev_a, prev_b, *, tiles_per_seq):
    i = pl.program_id(1)

    @pl.when(i % tiles_per_seq == 0)
    def _():
        prev_a[...] = jnp.zeros_like(prev_a)
        prev_b[...] = jnp.zeros_like(prev_b)

    h = h_ref[...]
    ua = jnp.dot(h, wa_ref[...], preferred_element_type=F32)
    ub = jnp.dot(h, wb_ref[...], preferred_element_type=F32)
    a = _causal_conv(ua, prev_a[...], cwa_ref[...], cba_ref[...])
    b = _causal_conv(ub, prev_b[...], cwb_ref[...], cbb_ref[...])
    o_ref[...] = (_silu(a) * b).astype(o_ref.dtype)
    prev_a[...] = ua[-SUBLANE:]
    prev_b[...] = ub[-SUBLANE:]


def _up_conv_gate(h, w_up, conv_w, conv_b, *, seq, name):
    m, d = h.shape
    f = w_up.shape[1] // 2
    tm, tn = 512, 512
    assert seq % tm == 0 and f % tn == 0
    nf = f // tn
    kw = conv_w.shape[0]
    need = 2 * (tm * d * 2 + 2 * d * tn * 2 + tm * tn * 2) + 8 * tm * tn * 4
    cb = conv_b.reshape(1, 2 * f)
    return pl.pallas_call(
        functools.partial(_up_conv_body, tiles_per_seq=seq // tm),
        grid=(nf, m // tm),
        in_specs=[pl.BlockSpec((tm, d), lambda j, i: (i, 0)),
                  pl.BlockSpec((d, tn), lambda j, i: (0, j)),
                  pl.BlockSpec((d, tn), lambda j, i: (0, j + nf)),
                  pl.BlockSpec((kw, tn), lambda j, i: (0, j)),
                  pl.BlockSpec((kw, tn), lambda j, i: (0, j + nf)),
                  pl.BlockSpec((1, tn), lambda j, i: (0, j)),
                  pl.BlockSpec((1, tn), lambda j, i: (0, j + nf))],
        out_specs=pl.BlockSpec((tm, tn), lambda j, i: (i, j)),
        out_shape=jax.ShapeDtypeStruct((m, f), BF16),
        scratch_shapes=[pltpu.VMEM((SUBLANE, tn), F32), pltpu.VMEM((SUBLANE, tn), F32)],
        compiler_params=_params(("parallel", "arbitrary"), need),
        name=name,
    )(h, w_up, w_up, conv_w, conv_w, cb, cb)


_NT = (((1,), (1,)), ((), ()))


def _softplus(x):
    return jnp.maximum(x, 0.0) + jnp.log1p(jnp.exp(-jnp.abs(x)))


def _conv_from_history(ext_ref, w, b):
    t = ext_ref.shape[0] - SUBLANE
    kw = w.shape[0]
    y = ext_ref[SUBLANE:, :] * w[kw - 1:kw, :] + b
    for s in range(1, kw):
        y = y + ext_ref[SUBLANE - s:SUBLANE - s + t, :] * w[kw - 1 - s:kw - s, :]
    return y


def _ssd_body(z_ref, xs_ref, b_ref, c_ref, dt_ref, hpar_ref, cwx_ref, cwb_ref, cwc_ref,
              cbx_ref, cbb_ref, cbc_ref, ng_ref, o_ref, state_ref, ex_ref, eb_ref, ec_ref,
              *, hg, hd, eps):
    g = pl.program_id(1)
    q, hp = xs_ref.shape
    hall = dt_ref.shape[1]
    assert 2 * hd == LANE and hp == hg * hd and hg % 2 == 0

    @pl.when(pl.program_id(2) == 0)
    def _():
        state_ref[...] = jnp.zeros_like(state_ref)
        for ext in (ex_ref, eb_ref, ec_ref):
            ext[:SUBLANE, :] = jnp.zeros((SUBLANE, ext.shape[1]), F32)

    ex_ref[SUBLANE:, :] = xs_ref[...]
    eb_ref[SUBLANE:, :] = b_ref[...]
    ec_ref[SUBLANE:, :] = c_ref[...]
    xs = _silu(_conv_from_history(ex_ref, cwx_ref[...], cbx_ref[...]))
    bm = _silu(_conv_from_history(eb_ref, cwb_ref[...], cbb_ref[...]))
    cm = _silu(_conv_from_history(ec_ref, cwc_ref[...], cbc_ref[...]))
    for ext in (ex_ref, eb_ref, ec_ref):
        ext[:SUBLANE, :] = ext[q:, :]

    hpar = hpar_ref[...]
    dt_all = _softplus(dt_ref[...] + hpar[0:1])
    da_all = dt_all * (-jnp.exp(hpar[1:2]))
    row = lax.broadcasted_iota(jnp.int32, (q, q), 0)
    col = lax.broadcasted_iota(jnp.int32, (q, q), 1)
    causal = col <= row
    ac3 = jnp.dot(causal.astype(BF16), jnp.concatenate(_split3(da_all), axis=1),
                  preferred_element_type=F32)
    acum_all = ac3[:, :hall] + ac3[:, hall:2 * hall] + ac3[:, 2 * hall:]

    head_of_chan = g * hg + lax.broadcasted_iota(jnp.int32, (hall, hp), 1) // hd
    expand = (lax.broadcasted_iota(jnp.int32, (hall, hp), 0) == head_of_chan).astype(BF16)
    pick = (lax.broadcasted_iota(jnp.int32, (hg, hall), 1)
            == g * hg + lax.broadcasted_iota(jnp.int32, (hg, hall), 0)).astype(BF16)
    stack = jnp.concatenate([dt_all, acum_all, hpar, hpar], axis=0)
    ns = stack.shape[0]
    sp3 = jnp.dot(jnp.concatenate(_split3(stack), axis=0), expand, preferred_element_type=F32)
    spread = sp3[:ns] + sp3[ns:2 * ns] + sp3[2 * ns:]
    dt_x, acum_x, d_x = spread[:q], spread[q:2 * q], spread[2 * q + 2:2 * q + 3]
    at3 = lax.dot_general(pick, jnp.concatenate(_split3(acum_all), axis=0), _NT,
                          preferred_element_type=F32)
    acum_t = at3[:, :q] + at3[:, q:2 * q] + at3[:, 2 * q:]
    last_x = acum_x[q - 1:q]

    xdt = xs * dt_x
    xdt_b = xdt.astype(BF16)
    cm_b = cm.astype(BF16)
    cb = lax.dot_general(cm_b, bm.astype(BF16), _NT, preferred_element_type=F32)

    lane_head = lax.broadcasted_iota(jnp.int32, (q, hall), 1)
    pair_rows = lax.broadcasted_iota(jnp.int32, (2 * q, LANE), 0) < q
    pair_lanes = lax.broadcasted_iota(jnp.int32, (2 * q, LANE), 1) < hd
    y_parts = []
    for pair in range(hg // 2):
        ms = []
        for h in (2 * pair, 2 * pair + 1):
            a_col = jnp.sum(jnp.where(lane_head == g * hg + h, acum_all, 0.0), axis=1,
                            keepdims=True)
            seg = a_col - acum_t[h:h + 1, :]
            ms.append((cb * jnp.exp(jnp.where(causal, seg, -jnp.inf))).astype(BF16))
        xk = xdt_b[:, pair * LANE:(pair + 1) * LANE]
        rhs = jnp.where(pair_rows == pair_lanes, jnp.concatenate([xk, xk], axis=0),
                        jnp.zeros((), BF16))
        y_parts.append(jnp.dot(jnp.concatenate(ms, axis=1), rhs, preferred_element_type=F32))
    y = jnp.concatenate(y_parts, axis=1)

    state = state_ref[...]
    y = y + jnp.dot(cm_b, state.astype(BF16), preferred_element_type=F32) * jnp.exp(acum_x)
    xw = (xdt * jnp.exp(last_x - acum_x)).astype(BF16)
    state_ref[...] = state * jnp.exp(last_x) + jnp.dot(
        bm.T.astype(BF16), xw, preferred_element_type=F32)

    y = y + d_x * xs
    yg = y * _silu(z_ref[...])
    yg = yg * lax.rsqrt(jnp.mean(yg * yg, axis=-1, keepdims=True) + eps)
    o_ref[...] = (yg * ng_ref[...]).astype(o_ref.dtype)


def _ssd_branch(p_ssd, dt_raw, hpar, conv_w, conv_b, norm_g, *, cfg, name):
    bsz, seq, _ = p_ssd.shape
    g_n, n, hd, q = cfg.ssd_groups, cfg.ssd_state, cfg.ssd_head_dim, cfg.ssd_chunk
    d_inner = norm_g.shape[0]
    hp = d_inner // g_n
    hg = hp // hd
    hall = hpar.shape[1]
    kw = conv_w.shape[0]
    xs0 = d_inner // hp
    b0 = 2 * d_inner // n
    c0 = b0 + g_n
    cwx, cwb, cwc = (conv_w[:, :d_inner], conv_w[:, d_inner:d_inner + g_n * n],
                     conv_w[:, d_inner + g_n * n:])
    cb2 = conv_b.reshape(1, -1)
    cbx, cbb, cbc = cb2[:, :d_inner], cb2[:, d_inner:d_inner + g_n * n], cb2[:, d_inner + g_n * n:]
    need = 2 * (2 * q * hp * 4 + 2 * q * n * 4 + q * hall * 4 + q * hp * 2) + 24 * q * hp * 4
    return pl.pallas_call(
        functools.partial(_ssd_body, hg=hg, hd=hd, eps=cfg.eps),
        grid=(bsz, g_n, seq // q),
        in_specs=[
            pl.BlockSpec((None, q, hp), lambda b, g, c: (b, c, g)),
            pl.BlockSpec((None, q, hp), lambda b, g, c: (b, c, xs0 + g)),
            pl.BlockSpec((None, q, n), lambda b, g, c: (b, c, b0 + g)),
            pl.BlockSpec((None, q, n), lambda b, g, c: (b, c, c0 + g)),
            pl.BlockSpec((None, q, hall), lambda b, g, c: (b, c, 0)),
            pl.BlockSpec((SUBLANE, hall), lambda b, g, c: (0, 0)),
            pl.BlockSpec((kw, hp), lambda b, g, c: (0, g)),
            pl.BlockSpec((kw, n), lambda b, g, c: (0, g)),
            pl.BlockSpec((kw, n), lambda b, g, c: (0, g)),
            pl.BlockSpec((1, hp), lambda b, g, c: (0, g)),
            pl.BlockSpec((1, n), lambda b, g, c: (0, g)),
            pl.BlockSpec((1, n), lambda b, g, c: (0, g)),
            pl.BlockSpec((1, hp), lambda b, g, c: (0, g)),
        ],
        out_specs=pl.BlockSpec((None, q, hp), lambda b, g, c: (b, c, g)),
        out_shape=jax.ShapeDtypeStruct((bsz, seq, d_inner), BF16),
        scratch_shapes=[pltpu.VMEM((n, hp), F32), pltpu.VMEM((SUBLANE + q, hp), F32),
                        pltpu.VMEM((SUBLANE + q, n), F32), pltpu.VMEM((SUBLANE + q, n), F32)],
        compiler_params=_params(("parallel", "parallel", "arbitrary"), need),
        name=name,
    )(p_ssd, p_ssd, p_ssd, p_ssd, dt_raw, hpar, cwx, cwb, cwc, cbx, cbb, cbc,
      norm_g.reshape(1, d_inner))


def _dsa_prep_body(k_ref, v_ref, ki_ref, kg_ref, kig_ref, kn_ref, vt_ref, kin_ref, *, hd, eps):
    for g in range(k_ref.shape[1] // hd):
        sl = slice(g * hd, (g + 1) * hd)
        kn_ref[:, sl] = (_rms(k_ref[:, sl], eps) * kg_ref[...]).astype(kn_ref.dtype)
        vt_ref[sl, :] = v_ref[:, sl].T.astype(vt_ref.dtype)
    kin_ref[...] = (_rms(ki_ref[...], eps) * kig_ref[...]).astype(kin_ref.dtype)


def _dsa_prep(p_att, p_small, k_norm_g, ki_norm_g, *, cfg, k_blk, v_blk, ki_blk, name):
    bsz, seq, _ = p_att.shape
    hd, idim, tk = cfg.att_head_dim, cfg.idx_dim, cfg.q_block
    kvw = cfg.att_kv_groups * hd
    nb = seq // tk
    need = 2 * (2 * tk * kvw * 4 + tk * idim * 4 + 2 * tk * kvw * 2 + tk * idim * 2) + 4 * tk * kvw * 4
    return pl.pallas_call(
        functools.partial(_dsa_prep_body, hd=hd, eps=cfg.eps),
        grid=(bsz, nb),
        in_specs=[pl.BlockSpec((None, tk, kvw), lambda b, i: (b, i, k_blk)),
                  pl.BlockSpec((None, tk, kvw), lambda b, i: (b, i, v_blk)),
                  pl.BlockSpec((None, tk, idim), lambda b, i: (b, i, ki_blk)),
                  pl.BlockSpec((1, hd), lambda b, i: (0, 0)),
                  pl.BlockSpec((1, idim), lambda b, i: (0, 0))],
        out_specs=[pl.BlockSpec((None, tk, kvw), lambda b, i: (b, i, 0)),
                   pl.BlockSpec((None, None, kvw, tk), lambda b, i: (b, i, 0, 0)),
                   pl.BlockSpec((None, tk, idim), lambda b, i: (b, i, 0))],
        out_shape=[jax.ShapeDtypeStruct((bsz, seq, kvw), BF16),
                   jax.ShapeDtypeStruct((bsz, nb, kvw, tk), BF16),
                   jax.ShapeDtypeStruct((bsz, seq, idim), BF16)],
        compiler_params=_params(("parallel", "parallel"), need),
        name=name,
    )(p_att, p_att, p_small, k_norm_g.reshape(1, hd), ki_norm_g.reshape(1, idim))


def _dsa_body(q_ref, qi_ref, wi_ref, kn_ref, vt_ref, kin_ref, qg_ref, o_ref,
              qn_s, qib_s, key_s, m_s, l_s, acc_s, sa_s, sb_s, ba_s, bb_s,
              *, nh, hd, kvg, ih, idim, ksel, eps):
    i = pl.program_id(1)
    tq = q_ref.shape[0]
    tk = tq
    hpg = nh // kvg
    hq = hpg * tq
    assert ih % hpg == 0

    for h in range(nh):
        qh = _rms(q_ref[:, h * hd:(h + 1) * hd], eps) * qg_ref[...] * (hd ** -0.5 * LOG2_E)
        qn_s[:, h * tq:(h + 1) * tq] = qh.T.astype(BF16)
    for h in range(ih):
        qib_s[:, h * tq:(h + 1) * tq] = qi_ref[:, h * idim:(h + 1) * idim].T.astype(BF16)
    wi_t = wi_ref[...].T * (idim ** -0.5 * ih ** -0.5)

    kpos = lax.broadcasted_iota(jnp.int32, (tk, tq), 0)
    qpos = lax.broadcasted_iota(jnp.int32, (tk, tq), 1)
    future = kpos > qpos

    def score_blk(kb, carry):
        ki_blk = kin_ref[pl.ds(pl.multiple_of(kb * tk, tk), tk), :]
        acc = jnp.zeros((tk, tq), F32)
        for h0 in range(0, ih, hpg):
            lt = jnp.dot(ki_blk, qib_s[:, h0 * tq:(h0 + hpg) * tq],
                         preferred_element_type=F32)
            for h in range(h0, h0 + hpg):
                acc = acc + wi_t[h:h + 1, :] * jnp.maximum(lt[:, (h - h0) * tq:(h - h0 + 1) * tq], 0.0)
        bits = pltpu.bitcast(acc, jnp.int32)
        key_s[kb] = bits ^ ((bits >> 31) & jnp.int32(0x7FFFFFFF))
        return carry

    lax.fori_loop(0, i + 1, score_blk, 0)
    key_s[i] = jnp.where(future, jnp.int32(INT_MIN), key_s[i])
    for u in range(1, COUNT_UNROLL):
        key_s[i + u] = jnp.full((tk, tq), INT_MIN, jnp.int32)
    n_steps = (i + COUNT_UNROLL) // COUNT_UNROLL

    def count_keys(accept):
        def step(j, cnt):
            for u in range(COUNT_UNROLL):
                cnt = cnt + jnp.where(accept(key_s[j * COUNT_UNROLL + u]), 1.0, 0.0)
            return cnt
        cnt = lax.fori_loop(0, n_steps, step, jnp.zeros((tk, tq), F32))
        return jnp.sum(cnt, axis=0, keepdims=True)

    def bit_step(bi, thr):
        cand = thr + (jnp.int32(1) << (31 - bi))
        return jnp.where(count_keys(lambda k: k >= cand) >= ksel, cand, thr)

    thr = lax.fori_loop(0, 32, bit_step, jnp.full((1, tq), INT_MIN, jnp.int32))

    ties_wanted = ksel - count_keys(lambda k: k > thr)
    earlier_key = (lax.broadcasted_iota(jnp.int32, (tk, tk), 1)
                   < lax.broadcasted_iota(jnp.int32, (tk, tk), 0)).astype(BF16)

    m_s[...] = jnp.full_like(m_s, NEG_BIG)
    l_s[...] = jnp.zeros_like(l_s)
    acc_s[...] = jnp.zeros_like(acc_s)

    def logits_into(s_ref, kb, g):
        row0 = pl.multiple_of(kb * tk, tk)
        s_ref[g] = jnp.dot(kn_ref[pl.ds(row0, tk), g * hd:(g + 1) * hd],
                           qn_s[:, g * hq:(g + 1) * hq], preferred_element_type=F32)

    def mask_into(bias_ref, kb, ties_before):
        keys = key_s[kb]
        tie = jnp.where(keys == thr, 1.0, 0.0)
        tie_rank = ties_before + jnp.dot(earlier_key, tie.astype(BF16),
                                         preferred_element_type=F32)
        keep_tie = jnp.where(tie_rank < ties_wanted, tie, 0.0)
        keep = jnp.where(keys > thr, 1.0, keep_tie)
        on_diag = (kb == i).astype(F32)
        keep = keep - jnp.where(future, on_diag, 0.0)
        bias_ref[...] = jnp.where(keep > 0.5, 0.0, NEG_BIG)
        return ties_before + jnp.sum(tie, axis=0, keepdims=True)

    def att_step(kb, ties_before, cur, nxt):
        (s_cur, bias_cur), (s_next, bias_next) = cur, nxt
        kb_next = jnp.minimum(kb + 1, i)
        bias = jnp.concatenate([bias_cur[...]] * hpg, axis=1)
        for g in range(kvg):
            logits_into(s_next, kb_next, g)
            vt_blk = vt_ref[kb, g * hd:(g + 1) * hd, :]
            s = s_cur[g] + bias
            m_old = m_s[g:g + 1, :]
            m_new = jnp.maximum(m_old, jnp.max(s, axis=0, keepdims=True))
            p = jnp.exp2(s - m_new)
            alpha = jnp.exp2(m_old - m_new)
            l_s[g:g + 1, :] = alpha * l_s[g:g + 1, :] + jnp.sum(p, axis=0, keepdims=True)
            acc_s[g * hd:(g + 1) * hd, :] = alpha * acc_s[g * hd:(g + 1) * hd, :] + jnp.dot(
                vt_blk, p.astype(BF16), preferred_element_type=F32)
            m_s[g:g + 1, :] = m_new
        return mask_into(bias_next, kb_next, ties_before)

    def att_blk(kb, ties_before):
        buf_a, buf_b = (sa_s, ba_s), (sb_s, bb_s)
        return lax.cond(kb % 2 == 0,
                        lambda t: att_step(kb, t, buf_a, buf_b),
                        lambda t: att_step(kb, t, buf_b, buf_a), ties_before)

    for g in range(kvg):
        logits_into(sa_s, 0, g)
    ties_before_1 = mask_into(ba_s, 0, jnp.zeros((1, tq), F32))
    lax.fori_loop(0, i + 1, att_blk, ties_before_1)

    for g in range(kvg):
        o_t = acc_s[g * hd:(g + 1) * hd, :] / l_s[g:g + 1, :]
        for hh in range(hpg):
            h = g * hpg + hh
            o_ref[:, h * hd:(h + 1) * hd] = o_t[:, hh * tq:(hh + 1) * tq].T.astype(o_ref.dtype)


def _dsa_attend(p_att, p_small, kn, vt, kin, q_norm_g, *, cfg, wi_blk, name):
    bsz, seq, _ = p_att.shape
    nh, hd, kvg = cfg.att_heads, cfg.att_head_dim, cfg.att_kv_groups
    ih, idim, tq = cfg.idx_heads, cfg.idx_dim, cfg.q_block
    assert nh * hd == ih * idim and ih <= LANE
    aw, kvw, nb, hpg = nh * hd, kvg * hd, seq // tq, nh // kvg
    ksel = min(cfg.topk_max, seq // 4)
    need = (2 * (2 * tq * aw * 4 + tq * LANE * 4 + 2 * seq * kvw * 2 + seq * idim * 2 + tq * aw * 2)
            + 2 * nh * tq * hd * 2 + seq * tq * 4 + 3 * nh * hd * tq * 4 + 16 * tq * tq * 4)
    return pl.pallas_call(
        functools.partial(_dsa_body, nh=nh, hd=hd, kvg=kvg, ih=ih, idim=idim, ksel=float(ksel),
                          eps=cfg.eps),
        grid=(bsz, nb),
        in_specs=[pl.BlockSpec((None, tq, aw), lambda b, i: (b, i, 0)),
                  pl.BlockSpec((None, tq, aw), lambda b, i: (b, i, 1)),
                  pl.BlockSpec((None, tq, LANE), lambda b, i: (b, i, wi_blk)),
                  pl.BlockSpec((None, seq, kvw), lambda b, i: (b, 0, 0)),
                  pl.BlockSpec((None, nb, kvw, tq), lambda b, i: (b, 0, 0, 0)),
                  pl.BlockSpec((None, seq, idim), lambda b, i: (b, 0, 0)),
                  pl.BlockSpec((1, hd), lambda b, i: (0, 0))],
        out_specs=pl.BlockSpec((None, tq, aw), lambda b, i: (b, i, 0)),
        out_shape=jax.ShapeDtypeStruct((bsz, seq, aw), BF16),
        scratch_shapes=[pltpu.VMEM((hd, nh * tq), BF16), pltpu.VMEM((idim, ih * tq), BF16),
                        pltpu.VMEM((nb + COUNT_UNROLL - 1, tq, tq), jnp.int32),
                        pltpu.VMEM((kvg, hpg * tq), F32), pltpu.VMEM((kvg, hpg * tq), F32),
                        pltpu.VMEM((kvg * hd, hpg * tq), F32),
                        pltpu.VMEM((kvg, tq, hpg * tq), F32), pltpu.VMEM((kvg, tq, hpg * tq), F32),
                        pltpu.VMEM((tq, tq), F32), pltpu.VMEM((tq, tq), F32)],
        compiler_params=_params(("parallel", "arbitrary"), need),
        name=name,
    )(p_att, p_att, p_small, kn, vt, kin, q_norm_g.reshape(1, hd))


def _pad_cols(w, width):
    return jnp.pad(w, ((0, 0), (0, width - w.shape[1])))


def _pack_w_in(w, cfg, d_model):
    d_inner = cfg.ssd_expand * d_model
    conv_ch = d_inner + 2 * cfg.ssd_groups * cfg.ssd_state
    heads = d_inner // cfg.ssd_head_dim
    aw = cfg.att_heads * cfg.att_head_dim
    kvw = cfg.att_kv_groups * cfg.att_head_dim
    sizes = (d_inner, conv_ch, heads, aw, kvw, kvw, cfg.idx_heads * cfg.idx_dim, cfg.idx_dim,
             cfg.idx_heads, d_model, d_model)
    assert sum(sizes) == w.shape[1]
    w = w.astype(BF16)
    parts, o = [], 0
    for s in sizes:
        parts.append(w[:, o:o + s])
        o += s
    z, xbc, dt, q, k, v, qi, ki, wi, g_ssd, g_att = parts
    hall = _round_up(heads, LANE)
    cat = functools.partial(jnp.concatenate, axis=1)
    return (cat([z, xbc]), cat([q, qi, k, v]), cat([g_ssd, g_att]),
            cat([_pad_cols(dt, hall), ki, _pad_cols(wi, LANE)]), hall)


def _gated(acc, g):
    return _sigmoid(g) * acc


def _gated_add(acc, g, prev):
    return prev + _sigmoid(g) * acc


def _residual(acc, x, gate, gate_t):
    return x + (gate + gate_t) * acc


def _forward(cfg, x, c, w_ada, b_ada, ada_table, norm1_g, w_in, ssd_conv_w, ssd_conv_b,
             ssd_dt_bias, ssd_a_log, ssd_d, ssd_norm_g, w_ssd_out, q_norm_g, k_norm_g,
             idx_k_norm_g, w_att_out, w_o, norm2_g, w_up, ffn_conv_w, ffn_conv_b, w_down):
    bsz, seq, d = x.shape
    m = bsz * seq
    depth = w_in.shape[0]
    aw = cfg.att_heads * cfg.att_head_dim
    kvw = cfg.att_kv_groups * cfg.att_head_dim
    heads = cfg.ssd_expand * d // cfg.ssd_head_dim

    mod = _ada_mod(c, w_ada, b_ada).reshape(bsz, cfg.n_mod, d)
    shift1, scale1, gate1, shift2, scale2, gate2 = [mod[:, i:i + 1, :] for i in range(cfg.n_mod)]
    x2 = x.reshape(m, d)
    for l in range(depth):
        tab = ada_table[l]
        w_ssd, w_att, w_gate, w_small, hall = _pack_w_in(w_in[l], cfg, d)
        hpar = jnp.zeros((SUBLANE, hall), F32)
        hpar = hpar.at[0, :heads].set(ssd_dt_bias[l]).at[1, :heads].set(ssd_a_log[l])
        hpar = hpar.at[2, :heads].set(ssd_d[l])

        h = _norm_mod(x2, norm1_g[l], scale1, tab[1], shift1, tab[0], seq=seq, eps=cfg.eps,
                      name=f"norm1_l{l}")
        p_ssd = _matmul(h, w_ssd, out_dtype=F32, name=f"in_ssd_l{l}").reshape(bsz, seq, -1)
        p_att = _matmul(h, w_att, out_dtype=F32, name=f"in_att_l{l}").reshape(bsz, seq, -1)
        p_gate = _matmul(h, w_gate, out_dtype=F32, name=f"in_gate_l{l}")
        p_small = _matmul(h, w_small, out_dtype=F32, name=f"in_small_l{l}").reshape(bsz, seq, -1)

        y_ssd = _ssd_branch(p_ssd, p_small, hpar, ssd_conv_w[l], ssd_conv_b[l], ssd_norm_g[l],
                            cfg=cfg, name=f"ssd_l{l}")
        kn, vt, kin = _dsa_prep(p_att, p_small, k_norm_g[l], idx_k_norm_g[l], cfg=cfg,
                                k_blk=2 * aw // kvw, v_blk=2 * aw // kvw + 1,
                                ki_blk=hall // cfg.idx_dim, name=f"dsa_prep_l{l}")
        o_att = _dsa_attend(p_att, p_small, kn, vt, kin, q_norm_g[l], cfg=cfg,
                            wi_blk=(hall + cfg.idx_dim) // LANE, name=f"dsa_l{l}")

        part = _matmul(y_ssd.reshape(m, -1), w_ssd_out[l].astype(BF16), out_dtype=F32,
                       epilogue=_gated, tiles=[(p_gate, 0)], name=f"ssd_out_l{l}")
        merged = _matmul(o_att.reshape(m, -1), w_att_out[l].astype(BF16), out_dtype=BF16,
                         epilogue=_gated_add, tiles=[(p_gate, d), (part, 0)],
                         name=f"att_out_l{l}")
        x2 = _matmul(merged, w_o[l].astype(BF16), out_dtype=F32, epilogue=_residual,
                     tiles=[(x2, 0)], rows=[(gate1, seq), (tab[2].reshape(1, 1, d), m)],
                     name=f"mix_out_l{l}")

        h = _norm_mod(x2, norm2_g[l], scale2, tab[4], shift2, tab[3], seq=seq, eps=cfg.eps,
                      name=f"norm2_l{l}")
        act = _up_conv_gate(h, w_up[l].astype(BF16), ffn_conv_w[l], ffn_conv_b[l], seq=seq,
                            name=f"ffn_up_l{l}")
        x2 = _matmul(act, w_down[l].astype(BF16), out_dtype=F32, epilogue=_residual,
                     tiles=[(x2, 0)], rows=[(gate2, seq), (tab[5].reshape(1, 1, d), m)],
                     name=f"ffn_down_l{l}")
    return x2.reshape(bsz, seq, d)


def kernel(x, c, w_ada, b_ada, ada_table, norm1_g, w_in, ssd_conv_w, ssd_conv_b, ssd_dt_bias,
           ssd_a_log, ssd_d, ssd_norm_g, w_ssd_out, q_norm_g, k_norm_g, idx_k_norm_g, w_att_out,
           w_o, norm2_g, w_up, ffn_conv_w, ffn_conv_b, w_down):
    return _forward(Cfg(), x, c, w_ada, b_ada, ada_table, norm1_g, w_in, ssd_conv_w, ssd_conv_b,
                    ssd_dt_bias, ssd_a_log, ssd_d, ssd_norm_g, w_ssd_out, q_norm_g, k_norm_g,
                    idx_k_norm_g, w_att_out, w_o, norm2_g, w_up, ffn_conv_w, ffn_conv_b, w_down)
```

```python
import functools
import math
from typing import NamedTuple

import jax
import jax.numpy as jnp
from jax import lax
from jax.experimental import pallas as pl
from jax.experimental.pallas import tpu as pltpu

V7X_VMEM_BYTES = 64 * 1024 * 1024
LANE = 128
SUBLANE = 8
VMEM_BUDGET = V7X_VMEM_BYTES - 8 * 1024 * 1024
COMPILER_SCRATCH = 4 * 1024 * 1024

F32 = jnp.float32
BF16 = jnp.bfloat16
NEG_BIG = -1e30
LOG2_E = math.log2(math.e)
COUNT_UNROLL = 4
INT_MIN = -(2 ** 31)


class Cfg(NamedTuple):
    ssd_head_dim: int = 64
    ssd_groups: int = 8
    ssd_state: int = 128
    ssd_conv: int = 4
    ssd_chunk: int = 128
    ssd_expand: int = 2
    att_heads: int = 32
    att_head_dim: int = 128
    att_kv_groups: int = 4
    idx_heads: int = 32
    idx_dim: int = 128
    topk_max: int = 256
    q_block: int = 128
    ffn_mult: int = 2
    ffn_conv: int = 3
    n_mod: int = 6
    eps: float = 1e-6


def _round_up(x, m):
    return (x + m - 1) // m * m


def _params(sem, vmem_bytes):
    return pltpu.CompilerParams(dimension_semantics=sem,
                                vmem_limit_bytes=int(min(vmem_bytes + COMPILER_SCRATCH, VMEM_BUDGET)))


def _sigmoid(x):
    return 0.5 * jnp.tanh(0.5 * x) + 0.5


def _silu(x):
    return x * _sigmoid(x)


def _split3(x):
    hi = x.astype(BF16)
    rest = x - hi.astype(F32)
    mid = rest.astype(BF16)
    return hi, mid, (rest - mid.astype(F32)).astype(BF16)


def _rms(x, eps):
    return x * lax.rsqrt(jnp.mean(x * x, axis=-1, keepdims=True) + eps)


def _mm_body(*refs, nk, n_extra, epilogue):
    a_ref, b_ref = refs[0], refs[1]
    extra = refs[2:2 + n_extra]
    o_ref = refs[2 + n_extra]
    part = jnp.dot(a_ref[...], b_ref[...], preferred_element_type=F32)
    if nk == 1:
        o_ref[...] = epilogue(part, *[e[...] for e in extra]).astype(o_ref.dtype)
        return
    acc_ref = refs[3 + n_extra]
    k = pl.program_id(2)

    @pl.when(k == 0)
    def _():
        acc_ref[...] = part

    @pl.when(k > 0)
    def _():
        acc_ref[...] += part

    @pl.when(k == nk - 1)
    def _():
        o_ref[...] = epilogue(acc_ref[...], *[e[...] for e in extra]).astype(o_ref.dtype)


def _mm_blocks(m, k, n, a_bytes, b_bytes, out_bytes, n_tile_extra, row_group):
    for tk in (k, 4096, 2048, 1024, 512):
        if tk > k or k % tk:
            continue
        for tm, tn in ((1024, 1024), (1024, 512), (512, 512), (1024, 128), (512, 256),
                       (256, 256), (256, 128), (128, 128)):
            if m % tm or n % tn or row_group % tm:
                continue
            windows = 2 * (tm * tk * a_bytes + tk * tn * b_bytes + tm * tn * out_bytes
                           + n_tile_extra * tm * tn * 4)
            temporaries = (3 if k > tk else 2) * tm * tn * 4
            need = windows + temporaries
            if need + COMPILER_SCRATCH <= VMEM_BUDGET:
                return tm, tn, tk, need
    raise ValueError(f"no matmul tiling for {(m, k, n)}")


def _identity(acc):
    return acc


def _matmul(a, b, *, out_dtype, epilogue=_identity, tiles=(), rows=(), name):
    m, k = a.shape
    n = b.shape[1]
    tm, tn, tk, need = _mm_blocks(m, k, n, a.dtype.itemsize, b.dtype.itemsize,
                                  jnp.dtype(out_dtype).itemsize, len(tiles),
                                  math.gcd(m, *[rpg for _, rpg in rows]))
    nk = k // tk
    in_specs = [pl.BlockSpec((tm, tk), lambda i, j, kk: (i, kk)),
                pl.BlockSpec((tk, tn), lambda i, j, kk: (kk, j))]
    for _, col0 in tiles:
        assert col0 % tn == 0
        in_specs.append(pl.BlockSpec((tm, tn), functools.partial(
            lambda i, j, kk, off: (i, j + off), off=col0 // tn)))
    for _, rows_per_group in rows:
        assert rows_per_group % tm == 0
        in_specs.append(pl.BlockSpec((None, 1, tn), functools.partial(
            lambda i, j, kk, tpg: (i // tpg, 0, j), tpg=rows_per_group // tm)))
    scratch = [pltpu.VMEM((tm, tn), F32)] if nk > 1 else []
    return pl.pallas_call(
        functools.partial(_mm_body, nk=nk, n_extra=len(tiles) + len(rows), epilogue=epilogue),
        grid=(m // tm, n // tn, nk),
        in_specs=in_specs,
        out_specs=pl.BlockSpec((tm, tn), lambda i, j, kk: (i, j)),
        out_shape=jax.ShapeDtypeStruct((m, n), out_dtype),
        scratch_shapes=scratch,
        compiler_params=_params(("parallel", "parallel", "arbitrary"), need),
        name=name,
    )(a, b, *[t for t, _ in tiles], *[r for r, _ in rows])


def _ada_body(c_ref, w_ref, b_ref, o_ref):
    part = jnp.dot(_silu(c_ref[...]), w_ref[...], preferred_element_type=F32)

    @pl.when(pl.program_id(0) == 0)
    def _():
        o_ref[...] = part + b_ref[...]

    @pl.when(pl.program_id(0) > 0)
    def _():
        o_ref[...] += part


def _ada_mod(c, w_ada, b_ada):
    bsz, d = c.shape
    n = w_ada.shape[1]
    mp = _round_up(bsz, SUBLANE)
    c_pad = jnp.zeros((mp, d), F32).at[:bsz].set(c)
    tk = d
    while tk % (2 * LANE) == 0 and tk * n * 4 > 12 * 2 ** 20:
        tk //= 2
    need = 2 * (mp * tk * 4 + tk * n * 4 + n * 4 + mp * n * 4) + 2 * mp * n * 4
    out = pl.pallas_call(
        _ada_body,
        grid=(d // tk,),
        in_specs=[pl.BlockSpec((mp, tk), lambda k: (0, k)),
                  pl.BlockSpec((tk, n), lambda k: (k, 0)),
                  pl.BlockSpec((1, n), lambda k: (0, 0))],
        out_specs=pl.BlockSpec((mp, n), lambda k: (0, 0)),
        out_shape=jax.ShapeDtypeStruct((mp, n), F32),
        compiler_params=_params(("arbitrary",), need),
        name="ada_mod",
    )(c_pad, w_ada, b_ada.reshape(1, n))
    return out[:bsz]


def _norm_mod_body(x_ref, g_ref, scale_ref, scale_t_ref, shift_ref, shift_t_ref, o_ref, *, eps):
    y = _rms(x_ref[...], eps) * g_ref[...]
    scale = scale_ref[...] + scale_t_ref[...]
    shift = shift_ref[...] + shift_t_ref[...]
    o_ref[...] = (y * (1.0 + scale) + shift).astype(o_ref.dtype)


def _norm_mod(x2, g, scale, scale_t, shift, shift_t, *, seq, eps, name):
    m, d = x2.shape
    tm = 256
    assert seq % tm == 0
    tiles_per_seq = seq // tm
    need = 2 * (tm * d * 4 + tm * d * 2 + 5 * d * 4) + 4 * tm * d * 4
    per_seq = pl.BlockSpec((None, 1, d), lambda i: (i // tiles_per_seq, 0, 0))
    shared = pl.BlockSpec((1, d), lambda i: (0, 0))
    return pl.pallas_call(
        functools.partial(_norm_mod_body, eps=eps),
        grid=(m // tm,),
        in_specs=[pl.BlockSpec((tm, d), lambda i: (i, 0)), shared, per_seq, shared, per_seq, shared],
        out_specs=pl.BlockSpec((tm, d), lambda i: (i, 0)),
        out_shape=jax.ShapeDtypeStruct((m, d), BF16),
        compiler_params=_params(("parallel",), need),
        name=name,
    )(x2, g.reshape(1, d), scale, scale_t.reshape(1, d), shift, shift_t.reshape(1, d))


def _shift_rows(x, prev8, s):
    rolled = pltpu.roll(x, s, 0)
    row = lax.broadcasted_iota(jnp.int32, (SUBLANE, x.shape[1]), 0)
    head = jnp.where(row < s, pltpu.roll(prev8, s, 0), rolled[:SUBLANE])
    return jnp.concatenate([head, rolled[SUBLANE:]], axis=0)


def _causal_conv(x, prev8, w, b):
    kw = w.shape[0]
    y = x * w[kw - 1:kw, :] + b
    for s in range(1, kw):
        y = y + _shift_rows(x, prev8, s) * w[kw - 1 - s:kw - s, :]
    return y


def _up_conv_body(h_ref, wa_ref, wb_ref, cwa_ref, cwb_ref, cba_ref, cbb_ref, o_ref,
                  prev_a, prev_b, *, tiles_per_seq):
    i = pl.program_id(1)

    @pl.when(i % tiles_per_seq == 0)
    def _():
        prev_a[...] = jnp.zeros_like(prev_a)
        prev_b[...] = jnp.zeros_like(prev_b)

    h = h_ref[...]
    ua = jnp.dot(h, wa_ref[...], preferred_element_type=F32)
    ub = jnp.dot(h, wb_ref[...], preferred_element_type=F32)
    a = _causal_conv(ua, prev_a[...], cwa_ref[...], cba_ref[...])
    b = _causal_conv(ub, prev_b[...], cwb_ref[...], cbb_ref[...])
    o_ref[...] = (_silu(a) * b).astype(o_ref.dtype)
    prev_a[...] = ua[-SUBLANE:]
    prev_b[...] = ub[-SUBLANE:]


def _up_conv_gate(h, w_up, conv_w, conv_b, *, seq, name):
    m, d = h.shape
    f = w_up.shape[1] // 2
    tm, tn = 512, 512
    assert seq % tm == 0 and f % tn == 0
    nf = f // tn
    kw = conv_w.shape[0]
    need = 2 * (tm * d * 2 + 2 * d * tn * 2 + tm * tn * 2) + 8 * tm * tn * 4
    cb = conv_b.reshape(1, 2 * f)
    return pl.pallas_call(
        functools.partial(_up_conv_body, tiles_per_seq=seq // tm),
        grid=(nf, m // tm),
        in_specs=[pl.BlockSpec((tm, d), lambda j, i: (i, 0)),
                  pl.BlockSpec((d, tn), lambda j, i: (0, j)),
                  pl.BlockSpec((d, tn), lambda j, i: (0, j + nf)),
                  pl.BlockSpec((kw, tn), lambda j, i: (0, j)),
                  pl.BlockSpec((kw, tn), lambda j, i: (0, j + nf)),
                  pl.BlockSpec((1, tn), lambda j, i: (0, j)),
                  pl.BlockSpec((1, tn), lambda j, i: (0, j + nf))],
        out_specs=pl.BlockSpec((tm, tn), lambda j, i: (i, j)),
        out_shape=jax.ShapeDtypeStruct((m, f), BF16),
        scratch_shapes=[pltpu.VMEM((SUBLANE, tn), F32), pltpu.VMEM((SUBLANE, tn), F32)],
        compiler_params=_params(("parallel", "arbitrary"), need),
        name=name,
    )(h, w_up, w_up, conv_w, conv_w, cb, cb)


_NT = (((1,), (1,)), ((), ()))


def _softplus(x):
    return jnp.maximum(x, 0.0) + jnp.log1p(jnp.exp(-jnp.abs(x)))


def _conv_from_history(ext_ref, w, b):
    t = ext_ref.shape[0] - SUBLANE
    kw = w.shape[0]
    y = ext_ref[SUBLANE:, :] * w[kw - 1:kw, :] + b
    for s in range(1, kw):
        y = y + ext_ref[SUBLANE - s:SUBLANE - s + t, :] * w[kw - 1 - s:kw - s, :]
    return y


def _ssd_body(z_ref, xs_ref, b_ref, c_ref, dt_ref, hpar_ref, cwx_ref, cwb_ref, cwc_ref,
              cbx_ref, cbb_ref, cbc_ref, ng_ref, o_ref, state_ref, ex_ref, eb_ref, ec_ref,
              *, hg, hd, eps):
    g = pl.program_id(1)
    q, hp = xs_ref.shape
    hall = dt_ref.shape[1]
    assert 2 * hd == LANE and hp == hg * hd and hg % 2 == 0

    @pl.when(pl.program_id(2) == 0)
    def _():
        state_ref[...] = jnp.zeros_like(state_ref)
        for ext in (ex_ref, eb_ref, ec_ref):
            ext[:SUBLANE, :] = jnp.zeros((SUBLANE, ext.shape[1]), F32)

    ex_ref[SUBLANE:, :] = xs_ref[...]
    eb_ref[SUBLANE:, :] = b_ref[...]
    ec_ref[SUBLANE:, :] = c_ref[...]
    xs = _silu(_conv_from_history(ex_ref, cwx_ref[...], cbx_ref[...]))
    bm = _silu(_conv_from_history(eb_ref, cwb_ref[...], cbb_ref[...]))
    cm = _silu(_conv_from_history(ec_ref, cwc_ref[...], cbc_ref[...]))
    for ext in (ex_ref, eb_ref, ec_ref):
        ext[:SUBLANE, :] = ext[q:, :]

    hpar = hpar_ref[...]
    dt_all = _softplus(dt_ref[...] + hpar[0:1])
    da_all = dt_all * (-jnp.exp(hpar[1:2]))
    row = lax.broadcasted_iota(jnp.int32, (q, q), 0)
    col = lax.broadcasted_iota(jnp.int32, (q, q), 1)
    causal = col <= row
    ac3 = jnp.dot(causal.astype(BF16), jnp.concatenate(_split3(da_all), axis=1),
                  preferred_element_type=F32)
    acum_all = ac3[:, :hall] + ac3[:, hall:2 * hall] + ac3[:, 2 * hall:]

    head_of_chan = g * hg + lax.broadcasted_iota(jnp.int32, (hall, hp), 1) // hd
    expand = (lax.broadcasted_iota(jnp.int32, (hall, hp), 0) == head_of_chan).astype(BF16)
    pick = (lax.broadcasted_iota(jnp.int32, (hg, hall), 1)
            == g * hg + lax.broadcasted_iota(jnp.int32, (hg, hall), 0)).astype(BF16)
    stack = jnp.concatenate([dt_all, acum_all, hpar, hpar], axis=0)
    ns = stack.shape[0]
    sp3 = jnp.dot(jnp.concatenate(_split3(stack), axis=0), expand, preferred_element_type=F32)
    spread = sp3[:ns] + sp3[ns:2 * ns] + sp3[2 * ns:]
    dt_x, acum_x, d_x = spread[:q], spread[q:2 * q], spread[2 * q + 2:2 * q + 3]
    at3 = lax.dot_general(pick, jnp.concatenate(_split3(acum_all), axis=0), _NT,
                          preferred_element_type=F32)
    acum_t = at3[:, :q] + at3[:, q:2 * q] + at3[:, 2 * q:]
    last_x = acum_x[q - 1:q]

    xdt = xs * dt_x
    xdt_b = xdt.astype(BF16)
    cm_b = cm.astype(BF16)
    cb = lax.dot_general(cm_b, bm.astype(BF16), _NT, preferred_element_type=F32)

    lane_head = lax.broadcasted_iota(jnp.int32, (q, hall), 1)
    pair_rows = lax.broadcasted_iota(jnp.int32, (2 * q, LANE), 0) < q
    pair_lanes = lax.broadcasted_iota(jnp.int32, (2 * q, LANE), 1) < hd
    y_parts = []
    for pair in range(hg // 2):
        ms = []
        for h in (2 * pair, 2 * pair + 1):
            a_col = jnp.sum(jnp.where(lane_head == g * hg + h, acum_all, 0.0), axis=1,
                            keepdims=True)
            seg = a_col - acum_t[h:h + 1, :]
            ms.append((cb * jnp.exp(jnp.where(causal, seg, -jnp.inf))).astype(BF16))
        xk = xdt_b[:, pair * LANE:(pair + 1) * LANE]
        rhs = jnp.where(pair_rows == pair_lanes, jnp.concatenate([xk, xk], axis=0),
                        jnp.zeros((), BF16))
        y_parts.append(jnp.dot(jnp.concatenate(ms, axis=1), rhs, preferred_element_type=F32))
    y = jnp.concatenate(y_parts, axis=1)

    state = state_ref[...]
    y = y + jnp.dot(cm_b, state.astype(BF16), preferred_element_type=F32) * jnp.exp(acum_x)
    xw = (xdt * jnp.exp(last_x - acum_x)).astype(BF16)
    state_ref[...] = state * jnp.exp(last_x) + jnp.dot(
        bm.T.astype(BF16), xw, preferred_element_type=F32)

    y = y + d_x * xs
    yg = y * _silu(z_ref[...])
    yg = yg * lax.rsqrt(jnp.mean(yg * yg, axis=-1, keepdims=True) + eps)
    o_ref[...] = (yg * ng_ref[...]).astype(o_ref.dtype)


def _ssd_branch(p_ssd, dt_raw, hpar, conv_w, conv_b, norm_g, *, cfg, name):
    bsz, seq, _ = p_ssd.shape
    g_n, n, hd, q = cfg.ssd_groups, cfg.ssd_state, cfg.ssd_head_dim, cfg.ssd_chunk
    d_inner = norm_g.shape[0]
    hp = d_inner // g_n
    hg = hp // hd
    hall = hpar.shape[1]
    kw = conv_w.shape[0]
    xs0 = d_inner // hp
    b0 = 2 * d_inner // n
    c0 = b0 + g_n
    cwx, cwb, cwc = (conv_w[:, :d_inner], conv_w[:, d_inner:d_inner + g_n * n],
                     conv_w[:, d_inner + g_n * n:])
    cb2 = conv_b.reshape(1, -1)
    cbx, cbb, cbc = cb2[:, :d_inner], cb2[:, d_inner:d_inner + g_n * n], cb2[:, d_inner + g_n * n:]
    need = 2 * (2 * q * hp * 4 + 2 * q * n * 4 + q * hall * 4 + q * hp * 2) + 24 * q * hp * 4
    return pl.pallas_call(
        functools.partial(_ssd_body, hg=hg, hd=hd, eps=cfg.eps),
        grid=(bsz, g_n, seq // q),
        in_specs=[
            pl.BlockSpec((None, q, hp), lambda b, g, c: (b, c, g)),
            pl.BlockSpec((None, q, hp), lambda b, g, c: (b, c, xs0 + g)),
            pl.BlockSpec((None, q, n), lambda b, g, c: (b, c, b0 + g)),
            pl.BlockSpec((None, q, n), lambda b, g, c: (b, c, c0 + g)),
            pl.BlockSpec((None, q, hall), lambda b, g, c: (b, c, 0)),
            pl.BlockSpec((SUBLANE, hall), lambda b, g, c: (0, 0)),
            pl.BlockSpec((kw, hp), lambda b, g, c: (0, g)),
            pl.BlockSpec((kw, n), lambda b, g, c: (0, g)),
            pl.BlockSpec((kw, n), lambda b, g, c: (0, g)),
            pl.BlockSpec((1, hp), lambda b, g, c: (0, g)),
            pl.BlockSpec((1, n), lambda b, g, c: (0, g)),
            pl.BlockSpec((1, n), lambda b, g, c: (0, g)),
            pl.BlockSpec((1, hp), lambda b, g, c: (0, g)),
        ],
        out_specs=pl.BlockSpec((None, q, hp), lambda b, g, c: (b, c, g)),
        out_shape=jax.ShapeDtypeStruct((bsz, seq, d_inner), BF16),
        scratch_shapes=[pltpu.VMEM((n, hp), F32), pltpu.VMEM((SUBLANE + q, hp), F32),
                        pltpu.VMEM((SUBLANE + q, n), F32), pltpu.VMEM((SUBLANE + q, n), F32)],
        compiler_params=_params(("parallel", "parallel", "arbitrary"), need),
        name=name,
    )(p_ssd, p_ssd, p_ssd, p_ssd, dt_raw, hpar, cwx, cwb, cwc, cbx, cbb, cbc,
      norm_g.reshape(1, d_inner))


def _dsa_prep_body(k_ref, v_ref, ki_ref, kg_ref, kig_ref, kn_ref, vt_ref, kin_ref, *, hd, eps):
    for g in range(k_ref.shape[1] // hd):
        sl = slice(g * hd, (g + 1) * hd)
        kn_ref[:, sl] = (_rms(k_ref[:, sl], eps) * kg_ref[...]).astype(kn_ref.dtype)
        vt_ref[sl, :] = v_ref[:, sl].T.astype(vt_ref.dtype)
    kin_ref[...] = (_rms(ki_ref[...], eps) * kig_ref[...]).astype(kin_ref.dtype)


def _dsa_prep(p_att, p_small, k_norm_g, ki_norm_g, *, cfg, k_blk, v_blk, ki_blk, name):
    bsz, seq, _ = p_att.shape
    hd, idim, tk = cfg.att_head_dim, cfg.idx_dim, cfg.q_block
    kvw = cfg.att_kv_groups * hd
    nb = seq // tk
    need = 2 * (2 * tk * kvw * 4 + tk * idim * 4 + 2 * tk * kvw * 2 + tk * idim * 2) + 4 * tk * kvw * 4
    return pl.pallas_call(
        functools.partial(_dsa_prep_body, hd=hd, eps=cfg.eps),
        grid=(bsz, nb),
        in_specs=[pl.BlockSpec((None, tk, kvw), lambda b, i: (b, i, k_blk)),
                  pl.BlockSpec((None, tk, kvw), lambda b, i: (b, i, v_blk)),
                  pl.BlockSpec((None, tk, idim), lambda b, i: (b, i, ki_blk)),
                  pl.BlockSpec((1, hd), lambda b, i: (0, 0)),
                  pl.BlockSpec((1, idim), lambda b, i: (0, 0))],
        out_specs=[pl.BlockSpec((None, tk, kvw), lambda b, i: (b, i, 0)),
                   pl.BlockSpec((None, None, kvw, tk), lambda b, i: (b, i, 0, 0)),
                   pl.BlockSpec((None, tk, idim), lambda b, i: (b, i, 0))],
        out_shape=[jax.ShapeDtypeStruct((bsz, seq, kvw), BF16),
                   jax.ShapeDtypeStruct((bsz, nb, kvw, tk), BF16),
                   jax.ShapeDtypeStruct((bsz, seq, idim), BF16)],
        compiler_params=_params(("parallel", "parallel"), need),
        name=name,
    )(p_att, p_att, p_small, k_norm_g.reshape(1, hd), ki_norm_g.reshape(1, idim))


def _dsa_body(q_ref, qi_ref, wi_ref, kn_ref, vt_ref, kin_ref, qg_ref, o_ref,
              qn_s, qib_s, key_s, m_s, l_s, acc_s, sa_s, sb_s, ba_s, bb_s,
              *, nh, hd, kvg, ih, idim, ksel, eps):
    i = pl.program_id(1)
    tq = q_ref.shape[0]
    tk = tq
    hpg = nh // kvg
    hq = hpg * tq
    assert ih % hpg == 0

    for h in range(nh):
        qh = _rms(q_ref[:, h * hd:(h + 1) * hd], eps) * qg_ref[...] * (hd ** -0.5 * LOG2_E)
        qn_s[:, h * tq:(h + 1) * tq] = qh.T.astype(BF16)
    for h in range(ih):
        qib_s[:, h * tq:(h + 1) * tq] = qi_ref[:, h * idim:(h + 1) * idim].T.astype(BF16)
    wi_t = wi_ref[...].T * (idim ** -0.5 * ih ** -0.5)

    kpos = lax.broadcasted_iota(jnp.int32, (tk, tq), 0)
    qpos = lax.broadcasted_iota(jnp.int32, (tk, tq), 1)
    future = kpos > qpos

    def score_blk(kb, carry):
        ki_blk = kin_ref[pl.ds(pl.multiple_of(kb * tk, tk), tk), :]
        acc = jnp.zeros((tk, tq), F32)
        for h0 in range(0, ih, hpg):
            lt = jnp.dot(ki_blk, qib_s[:, h0 * tq:(h0 + hpg) * tq],
                         preferred_element_type=F32)
            for h in range(h0, h0 + hpg):
                acc = acc + wi_t[h:h + 1, :] * jnp.maximum(lt[:, (h - h0) * tq:(h - h0 + 1) * tq], 0.0)
        bits = pltpu.bitcast(acc, jnp.int32)
        key_s[kb] = bits ^ ((bits >> 31) & jnp.int32(0x7FFFFFFF))
        return carry

    lax.fori_loop(0, i + 1, score_blk, 0)
    key_s[i] = jnp.where(future, jnp.int32(INT_MIN), key_s[i])
    for u in range(1, COUNT_UNROLL):
        key_s[i + u] = jnp.full((tk, tq), INT_MIN, jnp.int32)
    n_steps = (i + COUNT_UNROLL) // COUNT_UNROLL

    def count_keys(accept):
        def step(j, cnt):
            for u in range(COUNT_UNROLL):
                cnt = cnt + jnp.where(accept(key_s[j * COUNT_UNROLL + u]), 1.0, 0.0)
            return cnt
        cnt = lax.fori_loop(0, n_steps, step, jnp.zeros((tk, tq), F32))
        return jnp.sum(cnt, axis=0, keepdims=True)

    def bit_step(bi, thr):
        cand = thr + (jnp.int32(1) << (31 - bi))
        return jnp.where(count_keys(lambda k: k >= cand) >= ksel, cand, thr)

    thr = lax.fori_loop(0, 32, bit_step, jnp.full((1, tq), INT_MIN, jnp.int32))

    ties_wanted = ksel - count_keys(lambda k: k > thr)
    earlier_key = (lax.broadcasted_iota(jnp.int32, (tk, tk), 1)
                   < lax.broadcasted_iota(jnp.int32, (tk, tk), 0)).astype(BF16)

    m_s[...] = jnp.full_like(m_s, NEG_BIG)
    l_s[...] = jnp.zeros_like(l_s)
    acc_s[...] = jnp.zeros_like(acc_s)

    def logits_into(s_ref, kb, g):
        row0 = pl.multiple_of(kb * tk, tk)
        s_ref[g] = jnp.dot(kn_ref[pl.ds(row0, tk), g * hd:(g + 1) * hd],
                           qn_s[:, g * hq:(g + 1) * hq], preferred_element_type=F32)

    def mask_into(bias_ref, kb, ties_before):
        keys = key_s[kb]
        tie = jnp.where(keys == thr, 1.0, 0.0)
        tie_rank = ties_before + jnp.dot(earlier_key, tie.astype(BF16),
                                         preferred_element_type=F32)
        keep_tie = jnp.where(tie_rank < ties_wanted, tie, 0.0)
        keep = jnp.where(keys > thr, 1.0, keep_tie)
        on_diag = (kb == i).astype(F32)
        keep = keep - jnp.where(future, on_diag, 0.0)
        bias_ref[...] = jnp.where(keep > 0.5, 0.0, NEG_BIG)
        return ties_before + jnp.sum(tie, axis=0, keepdims=True)

    def att_step(kb, ties_before, cur, nxt):
        (s_cur, bias_cur), (s_next, bias_next) = cur, nxt
        kb_next = jnp.minimum(kb + 1, i)
        bias = jnp.concatenate([bias_cur[...]] * hpg, axis=1)
        for g in range(kvg):
            logits_into(s_next, kb_next, g)
            vt_blk = vt_ref[kb, g * hd:(g + 1) * hd, :]
            s = s_cur[g] + bias
            m_old = m_s[g:g + 1, :]
            m_new = jnp.maximum(m_old, jnp.max(s, axis=0, keepdims=True))
            p = jnp.exp2(s - m_new)
            alpha = jnp.exp2(m_old - m_new)
            l_s[g:g + 1, :] = alpha * l_s[g:g + 1, :] + jnp.sum(p, axis=0, keepdims=True)
            acc_s[g * hd:(g + 1) * hd, :] = alpha * acc_s[g * hd:(g + 1) * hd, :] + jnp.dot(
                vt_blk, p.astype(BF16), preferred_element_type=F32)
            m_s[g:g + 1, :] = m_new
        return mask_into(bias_next, kb_next, ties_before)

    def att_blk(kb, ties_before):
        buf_a, buf_b = (sa_s, ba_s), (sb_s, bb_s)
        return lax.cond(kb % 2 == 0,
                        lambda t: att_step(kb, t, buf_a, buf_b),
                        lambda t: att_step(kb, t, buf_b, buf_a), ties_before)

    for g in range(kvg):
        logits_into(sa_s, 0, g)
    ties_before_1 = mask_into(ba_s, 0, jnp.zeros((1, tq), F32))
    lax.fori_loop(0, i + 1, att_blk, ties_before_1)

    for g in range(kvg):
        o_t = acc_s[g * hd:(g + 1) * hd, :] / l_s[g:g + 1, :]
        for hh in range(hpg):
            h = g * hpg + hh
            o_ref[:, h * hd:(h + 1) * hd] = o_t[:, hh * tq:(hh + 1) * tq].T.astype(o_ref.dtype)


def _dsa_attend(p_att, p_small, kn, vt, kin, q_norm_g, *, cfg, wi_blk, name):
    bsz, seq, _ = p_att.shape
    nh, hd, kvg = cfg.att_heads, cfg.att_head_dim, cfg.att_kv_groups
    ih, idim, tq = cfg.idx_heads, cfg.idx_dim, cfg.q_block
    assert nh * hd == ih * idim and ih <= LANE
    aw, kvw, nb, hpg = nh * hd, kvg * hd, seq // tq, nh // kvg
    ksel = min(cfg.topk_max, seq // 4)
    need = (2 * (2 * tq * aw * 4 + tq * LANE * 4 + 2 * seq * kvw * 2 + seq * idim * 2 + tq * aw * 2)
            + 2 * nh * tq * hd * 2 + seq * tq * 4 + 3 * nh * hd * tq * 4 + 16 * tq * tq * 4)
    return pl.pallas_call(
        functools.partial(_dsa_body, nh=nh, hd=hd, kvg=kvg, ih=ih, idim=idim, ksel=float(ksel),
                          eps=cfg.eps),
        grid=(bsz, nb),
        in_specs=[pl.BlockSpec((None, tq, aw), lambda b, i: (b, i, 0)),
                  pl.BlockSpec((None, tq, aw), lambda b, i: (b, i, 1)),
                  pl.BlockSpec((None, tq, LANE), lambda b, i: (b, i, wi_blk)),
                  pl.BlockSpec((None, seq, kvw), lambda b, i: (b, 0, 0)),
                  pl.BlockSpec((None, nb, kvw, tq), lambda b, i: (b, 0, 0, 0)),
                  pl.BlockSpec((None, seq, idim), lambda b, i: (b, 0, 0)),
                  pl.BlockSpec((1, hd), lambda b, i: (0, 0))],
        out_specs=pl.BlockSpec((None, tq, aw), lambda b, i: (b, i, 0)),
        out_shape=jax.ShapeDtypeStruct((bsz, seq, aw), BF16),
        scratch_shapes=[pltpu.VMEM((hd, nh * tq), BF16), pltpu.VMEM((idim, ih * tq), BF16),
                        pltpu.VMEM((nb + COUNT_UNROLL - 1, tq, tq), jnp.int32),
                        pltpu.VMEM((kvg, hpg * tq), F32), pltpu.VMEM((kvg, hpg * tq), F32),
                        pltpu.VMEM((kvg * hd, hpg * tq), F32),
                        pltpu.VMEM((kvg, tq, hpg * tq), F32), pltpu.VMEM((kvg, tq, hpg * tq), F32),
                        pltpu.VMEM((tq, tq), F32), pltpu.VMEM((tq, tq), F32)],
        compiler_params=_params(("parallel", "arbitrary"), need),
        name=name,
    )(p_att, p_att, p_small, kn, vt, kin, q_norm_g.reshape(1, hd))


def _pad_cols(w, width):
    return jnp.pad(w, ((0, 0), (0, width - w.shape[1])))


def _pack_w_in(w, cfg, d_model):
    d_inner = cfg.ssd_expand * d_model
    conv_ch = d_inner + 2 * cfg.ssd_groups * cfg.ssd_state
    heads = d_inner // cfg.ssd_head_dim
    aw = cfg.att_heads * cfg.att_head_dim
    kvw = cfg.att_kv_groups * cfg.att_head_dim
    sizes = (d_inner, conv_ch, heads, aw, kvw, kvw, cfg.idx_heads * cfg.idx_dim, cfg.idx_dim,
             cfg.idx_heads, d_model, d_model)
    assert sum(sizes) == w.shape[1]
    parts, o = [], 0
    for s in sizes:
        parts.append(w[:, o:o + s])
        o += s
    _, _, dt, q, k, v, qi, ki, wi, _, _ = parts
    hall = _round_up(heads, LANE)
    cat = lambda ws: jnp.concatenate(ws, axis=1).astype(BF16)
    return (w[:, :d_inner + conv_ch].astype(BF16), cat([q, qi, k, v]),
            w[:, w.shape[1] - 2 * d_model:].astype(BF16),
            cat([_pad_cols(dt, hall), ki, _pad_cols(wi, LANE)]), hall)


def _gated(acc, g):
    return _sigmoid(g) * acc


def _gated_add(acc, g, prev):
    return prev + _sigmoid(g) * acc


def _residual(acc, x, gate, gate_t):
    return x + (gate + gate_t) * acc


def _forward(cfg, x, c, w_ada, b_ada, ada_table, norm1_g, w_in, ssd_conv_w, ssd_conv_b,
             ssd_dt_bias, ssd_a_log, ssd_d, ssd_norm_g, w_ssd_out, q_norm_g, k_norm_g,
             idx_k_norm_g, w_att_out, w_o, norm2_g, w_up, ffn_conv_w, ffn_conv_b, w_down):
    bsz, seq, d = x.shape
    m = bsz * seq
    depth = w_in.shape[0]
    aw = cfg.att_heads * cfg.att_head_dim
    kvw = cfg.att_kv_groups * cfg.att_head_dim
    heads = cfg.ssd_expand * d // cfg.ssd_head_dim

    mod = _ada_mod(c, w_ada, b_ada).reshape(bsz, cfg.n_mod, d)
    shift1, scale1, gate1, shift2, scale2, gate2 = [mod[:, i:i + 1, :] for i in range(cfg.n_mod)]
    x2 = x.reshape(m, d)
    for l in range(depth):
        tab = ada_table[l]
        w_ssd, w_att, w_gate, w_small, hall = _pack_w_in(w_in[l], cfg, d)
        hpar = jnp.zeros((SUBLANE, hall), F32)
        hpar = hpar.at[0, :heads].set(ssd_dt_bias[l]).at[1, :heads].set(ssd_a_log[l])
        hpar = hpar.at[2, :heads].set(ssd_d[l])

        h = _norm_mod(x2, norm1_g[l], scale1, tab[1], shift1, tab[0], seq=seq, eps=cfg.eps,
                      name=f"norm1_l{l}")
        p_ssd = _matmul(h, w_ssd, out_dtype=F32, name=f"in_ssd_l{l}").reshape(bsz, seq, -1)
        p_att = _matmul(h, w_att, out_dtype=F32, name=f"in_att_l{l}").reshape(bsz, seq, -1)
        p_gate = _matmul(h, w_gate, out_dtype=F32, name=f"in_gate_l{l}")
        p_small = _matmul(h, w_small, out_dtype=F32, name=f"in_small_l{l}").reshape(bsz, seq, -1)

        y_ssd = _ssd_branch(p_ssd, p_small, hpar, ssd_conv_w[l], ssd_conv_b[l], ssd_norm_g[l],
                            cfg=cfg, name=f"ssd_l{l}")
        kn, vt, kin = _dsa_prep(p_att, p_small, k_norm_g[l], idx_k_norm_g[l], cfg=cfg,
                                k_blk=2 * aw // kvw, v_blk=2 * aw // kvw + 1,
                                ki_blk=hall // cfg.idx_dim, name=f"dsa_prep_l{l}")
        o_att = _dsa_attend(p_att, p_small, kn, vt, kin, q_norm_g[l], cfg=cfg,
                            wi_blk=(hall + cfg.idx_dim) // LANE, name=f"dsa_l{l}")

        part = _matmul(y_ssd.reshape(m, -1), w_ssd_out[l].astype(BF16), out_dtype=F32,
                       epilogue=_gated, tiles=[(p_gate, 0)], name=f"ssd_out_l{l}")
        merged = _matmul(o_att.reshape(m, -1), w_att_out[l].astype(BF16), out_dtype=BF16,
                         epilogue=_gated_add, tiles=[(p_gate, d), (part, 0)],
                         name=f"att_out_l{l}")
        x2 = _matmul(merged, w_o[l].astype(BF16), out_dtype=F32, epilogue=_residual,
                     tiles=[(x2, 0)], rows=[(gate1, seq), (tab[2].reshape(1, 1, d), m)],
                     name=f"mix_out_l{l}")

        h = _norm_mod(x2, norm2_g[l], scale2, tab[4], shift2, tab[3], seq=seq, eps=cfg.eps,
                      name=f"norm2_l{l}")
        act = _up_conv_gate(h, w_up[l].astype(BF16), ffn_conv_w[l], ffn_conv_b[l], seq=seq,
                            name=f"ffn_up_l{l}")
        x2 = _matmul(act, w_down[l].astype(BF16), out_dtype=F32, epilogue=_residual,
                     tiles=[(x2, 0)], rows=[(gate2, seq), (tab[5].reshape(1, 1, d), m)],
                     name=f"ffn_down_l{l}")
    return x2.reshape(bsz, seq, d)


def kernel(x, c, w_ada, b_ada, ada_table, norm1_g, w_in, ssd_conv_w, ssd_conv_b, ssd_dt_bias,
           ssd_a_log, ssd_d, ssd_norm_g, w_ssd_out, q_norm_g, k_norm_g, idx_k_norm_g, w_att_out,
           w_o, norm2_g, w_up, ffn_conv_w, ffn_conv_b, w_down):
    return _forward(Cfg(), x, c, w_ada, b_ada, ada_table, norm1_g, w_in, ssd_conv_w, ssd_conv_b,
                    ssd_dt_bias, ssd_a_log, ssd_d, ssd_norm_g, w_ssd_out, q_norm_g, k_norm_g,
                    idx_k_norm_g, w_att_out, w_o, norm2_g, w_up, ffn_conv_w, ffn_conv_b, w_down)
```

```python
import functools
import math
from typing import NamedTuple

import jax
import jax.numpy as jnp
from jax import lax
from jax.experimental import pallas as pl
from jax.experimental.pallas import tpu as pltpu

V7X_VMEM_BYTES = 64 * 1024 * 1024
LANE = 128
SUBLANE = 8
VMEM_BUDGET = V7X_VMEM_BYTES - 8 * 1024 * 1024
COMPILER_SCRATCH = 4 * 1024 * 1024

F32 = jnp.float32
BF16 = jnp.bfloat16
NEG_BIG = -1e30
LOG2_E = math.log2(math.e)
COUNT_UNROLL = 4
INT_MIN = -(2 ** 31)


class Cfg(NamedTuple):
    ssd_head_dim: int = 64
    ssd_groups: int = 8
    ssd_state: int = 128
    ssd_conv: int = 4
    ssd_chunk: int = 128
    ssd_expand: int = 2
    att_heads: int = 32
    att_head_dim: int = 128
    att_kv_groups: int = 4
    idx_heads: int = 32
    idx_dim: int = 128
    topk_max: int = 256
    q_block: int = 128
    ffn_mult: int = 2
    ffn_conv: int = 3
    n_mod: int = 6
    eps: float = 1e-6


def _round_up(x, m):
    return (x + m - 1) // m * m


def _params(sem, vmem_bytes):
    return pltpu.CompilerParams(dimension_semantics=sem,
                                vmem_limit_bytes=int(min(vmem_bytes + COMPILER_SCRATCH, VMEM_BUDGET)))


def _sigmoid(x):
    return 0.5 * jnp.tanh(0.5 * x) + 0.5


def _silu(x):
    return x * _sigmoid(x)


def _split3(x):
    hi = x.astype(BF16)
    rest = x - hi.astype(F32)
    mid = rest.astype(BF16)
    return hi, mid, (rest - mid.astype(F32)).astype(BF16)


def _rms(x, eps):
    return x * lax.rsqrt(jnp.mean(x * x, axis=-1, keepdims=True) + eps)


def _mm_body(*refs, nk, n_extra, epilogue):
    a_ref, b_ref = refs[0], refs[1]
    extra = refs[2:2 + n_extra]
    o_ref = refs[2 + n_extra]
    part = jnp.dot(a_ref[...], b_ref[...], preferred_element_type=F32)
    if nk == 1:
        o_ref[...] = epilogue(part, *[e[...] for e in extra]).astype(o_ref.dtype)
        return
    acc_ref = refs[3 + n_extra]
    k = pl.program_id(2)

    @pl.when(k == 0)
    def _():
        acc_ref[...] = part

    @pl.when(k > 0)
    def _():
        acc_ref[...] += part

    @pl.when(k == nk - 1)
    def _():
        o_ref[...] = epilogue(acc_ref[...], *[e[...] for e in extra]).astype(o_ref.dtype)


def _mm_blocks(m, k, n, a_bytes, b_bytes, out_bytes, n_tile_extra, row_group):
    for tk in (k, 4096, 2048, 1024, 512):
        if tk > k or k % tk:
            continue
        for tm, tn in ((1024, 1024), (1024, 512), (512, 512), (1024, 128), (512, 256),
                       (256, 256), (256, 128), (128, 128)):
            if m % tm or n % tn or row_group % tm:
                continue
            windows = 2 * (tm * tk * a_bytes + tk * tn * b_bytes + tm * tn * out_bytes
                           + n_tile_extra * tm * tn * 4)
            temporaries = (3 if k > tk else 2) * tm * tn * 4
            need = windows + temporaries
            if need + COMPILER_SCRATCH <= VMEM_BUDGET:
                return tm, tn, tk, need
    raise ValueError(f"no matmul tiling for {(m, k, n)}")


def _identity(acc):
    return acc


def _matmul(a, b, *, out_dtype, epilogue=_identity, tiles=(), rows=(), name):
    m, k = a.shape
    n = b.shape[1]
    tm, tn, tk, need = _mm_blocks(m, k, n, a.dtype.itemsize, b.dtype.itemsize,
                                  jnp.dtype(out_dtype).itemsize, len(tiles),
                                  math.gcd(m, *[rpg for _, rpg in rows]))
    nk = k // tk
    in_specs = [pl.BlockSpec((tm, tk), lambda i, j, kk: (i, kk)),
                pl.BlockSpec((tk, tn), lambda i, j, kk: (kk, j))]
    for _, col0 in tiles:
        assert col0 % tn == 0
        in_specs.append(pl.BlockSpec((tm, tn), functools.partial(
            lambda i, j, kk, off: (i, j + off), off=col0 // tn)))
    for _, rows_per_group in rows:
        assert rows_per_group % tm == 0
        in_specs.append(pl.BlockSpec((None, 1, tn), functools.partial(
            lambda i, j, kk, tpg: (i // tpg, 0, j), tpg=rows_per_group // tm)))
    scratch = [pltpu.VMEM((tm, tn), F32)] if nk > 1 else []
    return pl.pallas_call(
        functools.partial(_mm_body, nk=nk, n_extra=len(tiles) + len(rows), epilogue=epilogue),
        grid=(m // tm, n // tn, nk),
        in_specs=in_specs,
        out_specs=pl.BlockSpec((tm, tn), lambda i, j, kk: (i, j)),
        out_shape=jax.ShapeDtypeStruct((m, n), out_dtype),
        scratch_shapes=scratch,
        compiler_params=_params(("parallel", "parallel", "arbitrary"), need),
        name=name,
    )(a, b, *[t for t, _ in tiles], *[r for r, _ in rows])


def _ada_body(c_ref, w_ref, b_ref, o_ref):
    part = jnp.dot(_silu(c_ref[...]), w_ref[...], preferred_element_type=F32)

    @pl.when(pl.program_id(0) == 0)
    def _():
        o_ref[...] = part + b_ref[...]

    @pl.when(pl.program_id(0) > 0)
    def _():
        o_ref[...] += part


def _ada_mod(c, w_ada, b_ada):
    bsz, d = c.shape
    n = w_ada.shape[1]
    mp = _round_up(bsz, SUBLANE)
    c_pad = jnp.zeros((mp, d), F32).at[:bsz].set(c)
    tk = d
    while tk % (2 * LANE) == 0 and tk * n * 4 > 12 * 2 ** 20:
        tk //= 2
    need = 2 * (mp * tk * 4 + tk * n * 4 + n * 4 + mp * n * 4) + 2 * mp * n * 4
    out = pl.pallas_call(
        _ada_body,
        grid=(d // tk,),
        in_specs=[pl.BlockSpec((mp, tk), lambda k: (0, k)),
                  pl.BlockSpec((tk, n), lambda k: (k, 0)),
                  pl.BlockSpec((1, n), lambda k: (0, 0))],
        out_specs=pl.BlockSpec((mp, n), lambda k: (0, 0)),
        out_shape=jax.ShapeDtypeStruct((mp, n), F32),
        compiler_params=_params(("arbitrary",), need),
        name="ada_mod",
    )(c_pad, w_ada, b_ada.reshape(1, n))
    return out[:bsz]


def _norm_mod_body(x_ref, g_ref, scale_ref, scale_t_ref, shift_ref, shift_t_ref, o_ref, *, eps):
    gain = g_ref[...] * (1.0 + scale_ref[...] + scale_t_ref[...])
    shift = shift_ref[...] + shift_t_ref[...]
    rows = 2 * SUBLANE

    def chunk(r, carry):
        sl = pl.ds(pl.multiple_of(r * rows, rows), rows)
        o_ref[sl, :] = (_rms(x_ref[sl, :], eps) * gain + shift).astype(o_ref.dtype)
        return carry

    lax.fori_loop(0, x_ref.shape[0] // rows, chunk, 0)


def _norm_mod(x2, g, scale, scale_t, shift, shift_t, *, seq, eps, name):
    m, d = x2.shape
    tm = 256
    assert seq % tm == 0
    tiles_per_seq = seq // tm
    need = 2 * (tm * d * 4 + tm * d * 2 + 5 * d * 4) + 4 * tm * d * 4
    per_seq = pl.BlockSpec((None, 1, d), lambda i: (i // tiles_per_seq, 0, 0))
    shared = pl.BlockSpec((1, d), lambda i: (0, 0))
    return pl.pallas_call(
        functools.partial(_norm_mod_body, eps=eps),
        grid=(m // tm,),
        in_specs=[pl.BlockSpec((tm, d), lambda i: (i, 0)), shared, per_seq, shared, per_seq, shared],
        out_specs=pl.BlockSpec((tm, d), lambda i: (i, 0)),
        out_shape=jax.ShapeDtypeStruct((m, d), BF16),
        compiler_params=_params(("parallel",), need),
        name=name,
    )(x2, g.reshape(1, d), scale, scale_t.reshape(1, d), shift, shift_t.reshape(1, d))


def _shift_rows(x, prev8, s):
    rolled = pltpu.roll(x, s, 0)
    row = lax.broadcasted_iota(jnp.int32, (SUBLANE, x.shape[1]), 0)
    head = jnp.where(row < s, pltpu.roll(prev8, s, 0), rolled[:SUBLANE])
    return jnp.concatenate([head, rolled[SUBLANE:]], axis=0)


def _causal_conv(x, prev8, w, b):
    kw = w.shape[0]
    y = x * w[kw - 1:kw, :] + b
    for s in range(1, kw):
        y = y + _shift_rows(x, prev8, s) * w[kw - 1 - s:kw - s, :]
    return y


def _up_conv_body(h_ref, wa_ref, wb_ref, cwa_ref, cwb_ref, cba_ref, cbb_ref, o_ref,
                  prev_a, prev_b, *, tiles_per_seq):
    i = pl.program_id(1)

    @pl.when(i % tiles_per_seq == 0)
    def _():
        prev_a[...] = jnp.zeros_like(prev_a)
        prev_b[...] = jnp.zeros_like(prev_b)

    h = h_ref[...]
    ua = jnp.dot(h, wa_ref[...], preferred_element_type=F32)
    ub = jnp.dot(h, wb_ref[...], preferred_element_type=F32)
    a = _causal_conv(ua, prev_a[...], cwa_ref[...], cba_ref[...])
    b = _causal_conv(ub, prev_b[...], cwb_ref[...], cbb_ref[...])
    o_ref[...] = (_silu(a) * b).astype(o_ref.dtype)
    prev_a[...] = ua[-SUBLANE:]
    prev_b[...] = ub[-SUBLANE:]


def _up_conv_gate(h, w_up, conv_w, conv_b, *, seq, name):
    m, d = h.shape
    f = w_up.shape[1] // 2
    tm, tn = 512, 512
    assert seq % tm == 0 and f % tn == 0
    nf = f // tn
    kw = conv_w.shape[0]
    need = 2 * (tm * d * 2 + 2 * d * tn * 2 + tm * tn * 2) + 8 * tm * tn * 4
    cb = conv_b.reshape(1, 2 * f)
    return pl.pallas_call(
        functools.partial(_up_conv_body, tiles_per_seq=seq // tm),
        grid=(nf, m // tm),
        in_specs=[pl.BlockSpec((tm, d), lambda j, i: (i, 0)),
                  pl.BlockSpec((d, tn), lambda j, i: (0, j)),
                  pl.BlockSpec((d, tn), lambda j, i: (0, j + nf)),
                  pl.BlockSpec((kw, tn), lambda j, i: (0, j)),
                  pl.BlockSpec((kw, tn), lambda j, i: (0, j + nf)),
                  pl.BlockSpec((1, tn), lambda j, i: (0, j)),
                  pl.BlockSpec((1, tn), lambda j, i: (0, j + nf))],
        out_specs=pl.BlockSpec((tm, tn), lambda j, i: (i, j)),
        out_shape=jax.ShapeDtypeStruct((m, f), BF16),
        scratch_shapes=[pltpu.VMEM((SUBLANE, tn), F32), pltpu.VMEM((SUBLANE, tn), F32)],
        compiler_params=_params(("parallel", "arbitrary"), need),
        name=name,
    )(h, w_up, w_up, conv_w, conv_w, cb, cb)


_NT = (((1,), (1,)), ((), ()))


def _softplus(x):
    return jnp.maximum(x, 0.0) + jnp.log1p(jnp.exp(-jnp.abs(x)))


def _conv_from_history(ext_ref, w, b):
    t = ext_ref.shape[0] - SUBLANE
    kw = w.shape[0]
    y = ext_ref[SUBLANE:, :] * w[kw - 1:kw, :] + b
    for s in range(1, kw):
        y = y + ext_ref[SUBLANE - s:SUBLANE - s + t, :] * w[kw - 1 - s:kw - s, :]
    return y


def _ssd_body(z_ref, xs_ref, b_ref, c_ref, dt_ref, hpar_ref, cwx_ref, cwb_ref, cwc_ref,
              cbx_ref, cbb_ref, cbc_ref, ng_ref, o_ref, state_ref, ex_ref, eb_ref, ec_ref,
              *, hg, hd, eps):
    g = pl.program_id(1)
    q, hp = xs_ref.shape
    hall = dt_ref.shape[1]
    assert 2 * hd == LANE and hp == hg * hd and hg % 2 == 0

    @pl.when(pl.program_id(2) == 0)
    def _():
        state_ref[...] = jnp.zeros_like(state_ref)
        for ext in (ex_ref, eb_ref, ec_ref):
            ext[:SUBLANE, :] = jnp.zeros((SUBLANE, ext.shape[1]), F32)

    ex_ref[SUBLANE:, :] = xs_ref[...]
    eb_ref[SUBLANE:, :] = b_ref[...]
    ec_ref[SUBLANE:, :] = c_ref[...]
    xs = _silu(_conv_from_history(ex_ref, cwx_ref[...], cbx_ref[...]))
    bm = _silu(_conv_from_history(eb_ref, cwb_ref[...], cbb_ref[...]))
    cm = _silu(_conv_from_history(ec_ref, cwc_ref[...], cbc_ref[...]))
    for ext in (ex_ref, eb_ref, ec_ref):
        ext[:SUBLANE, :] = ext[q:, :]

    hpar = hpar_ref[...]
    dt_all = _softplus(dt_ref[...] + hpar[0:1])
    da_all = dt_all * (-jnp.exp(hpar[1:2]))
    row = lax.broadcasted_iota(jnp.int32, (q, q), 0)
    col = lax.broadcasted_iota(jnp.int32, (q, q), 1)
    causal = col <= row
    ac3 = jnp.dot(causal.astype(BF16), jnp.concatenate(_split3(da_all), axis=1),
                  preferred_element_type=F32)
    acum_all = ac3[:, :hall] + ac3[:, hall:2 * hall] + ac3[:, 2 * hall:]

    head_of_chan = g * hg + lax.broadcasted_iota(jnp.int32, (hall, hp), 1) // hd
    expand = (lax.broadcasted_iota(jnp.int32, (hall, hp), 0) == head_of_chan).astype(BF16)
    pick = (lax.broadcasted_iota(jnp.int32, (hg, hall), 1)
            == g * hg + lax.broadcasted_iota(jnp.int32, (hg, hall), 0)).astype(BF16)
    stack = jnp.concatenate([dt_all, acum_all, hpar, hpar], axis=0)
    ns = stack.shape[0]
    sp3 = jnp.dot(jnp.concatenate(_split3(stack), axis=0), expand, preferred_element_type=F32)
    spread = sp3[:ns] + sp3[ns:2 * ns] + sp3[2 * ns:]
    dt_x, acum_x, d_x = spread[:q], spread[q:2 * q], spread[2 * q + 2:2 * q + 3]
    at3 = lax.dot_general(pick, jnp.concatenate(_split3(acum_all), axis=0), _NT,
                          preferred_element_type=F32)
    acum_t = at3[:, :q] + at3[:, q:2 * q] + at3[:, 2 * q:]
    last_x = acum_x[q - 1:q]

    xdt = xs * dt_x
    xdt_b = xdt.astype(BF16)
    cm_b = cm.astype(BF16)
    cb = lax.dot_general(cm_b, bm.astype(BF16), _NT, preferred_element_type=F32)

    lane_head = lax.broadcasted_iota(jnp.int32, (q, hall), 1)
    pair_rows = lax.broadcasted_iota(jnp.int32, (2 * q, LANE), 0) < q
    pair_lanes = lax.broadcasted_iota(jnp.int32, (2 * q, LANE), 1) < hd
    y_parts = []
    for pair in range(hg // 2):
        ms = []
        for h in (2 * pair, 2 * pair + 1):
            a_col = jnp.sum(jnp.where(lane_head == g * hg + h, acum_all, 0.0), axis=1,
                            keepdims=True)
            seg = a_col - acum_t[h:h + 1, :]
            ms.append((cb * jnp.exp(jnp.where(causal, seg, -jnp.inf))).astype(BF16))
        xk = xdt_b[:, pair * LANE:(pair + 1) * LANE]
        rhs = jnp.where(pair_rows == pair_lanes, jnp.concatenate([xk, xk], axis=0),
                        jnp.zeros((), BF16))
        y_parts.append(jnp.dot(jnp.concatenate(ms, axis=1), rhs, preferred_element_type=F32))
    y = jnp.concatenate(y_parts, axis=1)

    state = state_ref[...]
    y = y + jnp.dot(cm_b, state.astype(BF16), preferred_element_type=F32) * jnp.exp(acum_x)
    xw = (xdt * jnp.exp(last_x - acum_x)).astype(BF16)
    state_ref[...] = state * jnp.exp(last_x) + jnp.dot(
        bm.T.astype(BF16), xw, preferred_element_type=F32)

    y = y + d_x * xs
    yg = y * _silu(z_ref[...])
    yg = yg * lax.rsqrt(jnp.mean(yg * yg, axis=-1, keepdims=True) + eps)
    o_ref[...] = (yg * ng_ref[...]).astype(o_ref.dtype)


def _ssd_branch(p_ssd, dt_raw, hpar, conv_w, conv_b, norm_g, *, cfg, name):
    bsz, seq, _ = p_ssd.shape
    g_n, n, hd, q = cfg.ssd_groups, cfg.ssd_state, cfg.ssd_head_dim, cfg.ssd_chunk
    d_inner = norm_g.shape[0]
    hp = d_inner // g_n
    hg = hp // hd
    hall = hpar.shape[1]
    kw = conv_w.shape[0]
    xs0 = d_inner // hp
    b0 = 2 * d_inner // n
    c0 = b0 + g_n
    cwx, cwb, cwc = (conv_w[:, :d_inner], conv_w[:, d_inner:d_inner + g_n * n],
                     conv_w[:, d_inner + g_n * n:])
    cb2 = conv_b.reshape(1, -1)
    cbx, cbb, cbc = cb2[:, :d_inner], cb2[:, d_inner:d_inner + g_n * n], cb2[:, d_inner + g_n * n:]
    need = 2 * (2 * q * hp * 4 + 2 * q * n * 4 + q * hall * 4 + q * hp * 2) + 24 * q * hp * 4
    return pl.pallas_call(
        functools.partial(_ssd_body, hg=hg, hd=hd, eps=cfg.eps),
        grid=(bsz, g_n, seq // q),
        in_specs=[
            pl.BlockSpec((None, q, hp), lambda b, g, c: (b, c, g)),
            pl.BlockSpec((None, q, hp), lambda b, g, c: (b, c, xs0 + g)),
            pl.BlockSpec((None, q, n), lambda b, g, c: (b, c, b0 + g)),
            pl.BlockSpec((None, q, n), lambda b, g, c: (b, c, c0 + g)),
            pl.BlockSpec((None, q, hall), lambda b, g, c: (b, c, 0)),
            pl.BlockSpec((SUBLANE, hall), lambda b, g, c: (0, 0)),
            pl.BlockSpec((kw, hp), lambda b, g, c: (0, g)),
            pl.BlockSpec((kw, n), lambda b, g, c: (0, g)),
            pl.BlockSpec((kw, n), lambda b, g, c: (0, g)),
            pl.BlockSpec((1, hp), lambda b, g, c: (0, g)),
            pl.BlockSpec((1, n), lambda b, g, c: (0, g)),
            pl.BlockSpec((1, n), lambda b, g, c: (0, g)),
            pl.BlockSpec((1, hp), lambda b, g, c: (0, g)),
        ],
        out_specs=pl.BlockSpec((None, q, hp), lambda b, g, c: (b, c, g)),
        out_shape=jax.ShapeDtypeStruct((bsz, seq, d_inner), BF16),
        scratch_shapes=[pltpu.VMEM((n, hp), F32), pltpu.VMEM((SUBLANE + q, hp), F32),
                        pltpu.VMEM((SUBLANE + q, n), F32), pltpu.VMEM((SUBLANE + q, n), F32)],
        compiler_params=_params(("parallel", "parallel", "arbitrary"), need),
        name=name,
    )(p_ssd, p_ssd, p_ssd, p_ssd, dt_raw, hpar, cwx, cwb, cwc, cbx, cbb, cbc,
      norm_g.reshape(1, d_inner))


def _dsa_prep_body(k_ref, v_ref, ki_ref, kg_ref, kig_ref, kn_ref, vt_ref, kin_ref, *, hd, eps):
    for g in range(k_ref.shape[1] // hd):
        sl = slice(g * hd, (g + 1) * hd)
        kn_ref[:, sl] = (_rms(k_ref[:, sl], eps) * kg_ref[...]).astype(kn_ref.dtype)
        vt_ref[sl, :] = v_ref[:, sl].T.astype(vt_ref.dtype)
    kin_ref[...] = (_rms(ki_ref[...], eps) * kig_ref[...]).astype(kin_ref.dtype)


def _dsa_prep(p_att, p_small, k_norm_g, ki_norm_g, *, cfg, k_blk, v_blk, ki_blk, name):
    bsz, seq, _ = p_att.shape
    hd, idim, tk = cfg.att_head_dim, cfg.idx_dim, cfg.q_block
    kvw = cfg.att_kv_groups * hd
    nb = seq // tk
    need = 2 * (2 * tk * kvw * 4 + tk * idim * 4 + 2 * tk * kvw * 2 + tk * idim * 2) + 4 * tk * kvw * 4
    return pl.pallas_call(
        functools.partial(_dsa_prep_body, hd=hd, eps=cfg.eps),
        grid=(bsz, nb),
        in_specs=[pl.BlockSpec((None, tk, kvw), lambda b, i: (b, i, k_blk)),
                  pl.BlockSpec((None, tk, kvw), lambda b, i: (b, i, v_blk)),
                  pl.BlockSpec((None, tk, idim), lambda b, i: (b, i, ki_blk)),
                  pl.BlockSpec((1, hd), lambda b, i: (0, 0)),
                  pl.BlockSpec((1, idim), lambda b, i: (0, 0))],
        out_specs=[pl.BlockSpec((None, tk, kvw), lambda b, i: (b, i, 0)),
                   pl.BlockSpec((None, None, kvw, tk), lambda b, i: (b, i, 0, 0)),
                   pl.BlockSpec((None, tk, idim), lambda b, i: (b, i, 0))],
        out_shape=[jax.ShapeDtypeStruct((bsz, seq, kvw), BF16),
                   jax.ShapeDtypeStruct((bsz, nb, kvw, tk), BF16),
                   jax.ShapeDtypeStruct((bsz, seq, idim), BF16)],
        compiler_params=_params(("parallel", "parallel"), need),
        name=name,
    )(p_att, p_att, p_small, k_norm_g.reshape(1, hd), ki_norm_g.reshape(1, idim))


def _dsa_body(q_ref, qi_ref, wi_ref, kn_ref, vt_ref, kin_ref, qg_ref, o_ref,
              qn_s, qib_s, key_s, m_s, l_s, acc_s, sa_s, sb_s, ba_s, bb_s,
              *, nh, hd, kvg, ih, idim, ksel, eps):
    i = pl.program_id(1)
    tq = q_ref.shape[0]
    tk = tq
    hpg = nh // kvg
    hq = hpg * tq
    assert ih % hpg == 0

    for h in range(nh):
        qh = _rms(q_ref[:, h * hd:(h + 1) * hd], eps) * qg_ref[...] * (hd ** -0.5 * LOG2_E)
        qn_s[:, h * tq:(h + 1) * tq] = qh.T.astype(BF16)
    for h in range(ih):
        qib_s[:, h * tq:(h + 1) * tq] = qi_ref[:, h * idim:(h + 1) * idim].T.astype(BF16)
    wi_t = wi_ref[...].T * (idim ** -0.5 * ih ** -0.5)

    kpos = lax.broadcasted_iota(jnp.int32, (tk, tq), 0)
    qpos = lax.broadcasted_iota(jnp.int32, (tk, tq), 1)
    future = kpos > qpos

    def score_blk(kb, carry):
        ki_blk = kin_ref[pl.ds(pl.multiple_of(kb * tk, tk), tk), :]
        acc = jnp.zeros((tk, tq), F32)
        for h0 in range(0, ih, hpg):
            lt = jnp.dot(ki_blk, qib_s[:, h0 * tq:(h0 + hpg) * tq],
                         preferred_element_type=F32)
            for h in range(h0, h0 + hpg):
                acc = acc + wi_t[h:h + 1, :] * jnp.maximum(lt[:, (h - h0) * tq:(h - h0 + 1) * tq], 0.0)
        bits = pltpu.bitcast(acc, jnp.int32)
        key_s[kb] = bits ^ ((bits >> 31) & jnp.int32(0x7FFFFFFF))
        return carry

    lax.fori_loop(0, i + 1, score_blk, 0)
    key_s[i] = jnp.where(future, jnp.int32(INT_MIN), key_s[i])
    for u in range(1, COUNT_UNROLL):
        key_s[i + u] = jnp.full((tk, tq), INT_MIN, jnp.int32)
    n_steps = (i + COUNT_UNROLL) // COUNT_UNROLL

    def count_keys(accept):
        def step(j, cnt):
            for u in range(COUNT_UNROLL):
                cnt = cnt + jnp.where(accept(key_s[j * COUNT_UNROLL + u]), 1.0, 0.0)
            return cnt
        cnt = lax.fori_loop(0, n_steps, step, jnp.zeros((tk, tq), F32))
        return jnp.sum(cnt, axis=0, keepdims=True)

    def bit_step(bi, thr):
        cand = thr + (jnp.int32(1) << (31 - bi))
        return jnp.where(count_keys(lambda k: k >= cand) >= ksel, cand, thr)

    thr = lax.fori_loop(0, 32, bit_step, jnp.full((1, tq), INT_MIN, jnp.int32))

    ties_wanted = ksel - count_keys(lambda k: k > thr)
    earlier_key = (lax.broadcasted_iota(jnp.int32, (tk, tk), 1)
                   < lax.broadcasted_iota(jnp.int32, (tk, tk), 0)).astype(BF16)

    m_s[...] = jnp.full_like(m_s, NEG_BIG)
    l_s[...] = jnp.zeros_like(l_s)
    acc_s[...] = jnp.zeros_like(acc_s)

    def logits_into(s_ref, kb, g):
        row0 = pl.multiple_of(kb * tk, tk)
        s_ref[g] = jnp.dot(kn_ref[pl.ds(row0, tk), g * hd:(g + 1) * hd],
                           qn_s[:, g * hq:(g + 1) * hq], preferred_element_type=F32)

    def mask_into(bias_ref, kb, ties_before):
        keys = key_s[kb]
        tie = jnp.where(keys == thr, 1.0, 0.0)
        tie_rank = ties_before + jnp.dot(earlier_key, tie.astype(BF16),
                                         preferred_element_type=F32)
        keep_tie = jnp.where(tie_rank < ties_wanted, tie, 0.0)
        keep = jnp.where(keys > thr, 1.0, keep_tie)
        on_diag = (kb == i).astype(F32)
        keep = keep - jnp.where(future, on_diag, 0.0)
        bias_ref[...] = jnp.where(keep > 0.5, 0.0, NEG_BIG)
        return ties_before + jnp.sum(tie, axis=0, keepdims=True)

    def att_step(kb, ties_before, cur, nxt):
        (s_cur, bias_cur), (s_next, bias_next) = cur, nxt
        kb_next = jnp.minimum(kb + 1, i)
        bias = jnp.concatenate([bias_cur[...]] * hpg, axis=1)
        for g in range(kvg):
            logits_into(s_next, kb_next, g)
            vt_blk = vt_ref[kb, g * hd:(g + 1) * hd, :]
            s = s_cur[g] + bias
            m_old = m_s[g:g + 1, :]
            m_new = jnp.maximum(m_old, jnp.max(s, axis=0, keepdims=True))
            p = jnp.exp2(s - m_new)
            alpha = jnp.exp2(m_old - m_new)
            l_s[g:g + 1, :] = alpha * l_s[g:g + 1, :] + jnp.sum(p, axis=0, keepdims=True)
            acc_s[g * hd:(g + 1) * hd, :] = alpha * acc_s[g * hd:(g + 1) * hd, :] + jnp.dot(
                vt_blk, p.astype(BF16), preferred_element_type=F32)
            m_s[g:g + 1, :] = m_new
        return mask_into(bias_next, kb_next, ties_before)

    def att_blk(kb, ties_before):
        buf_a, buf_b = (sa_s, ba_s), (sb_s, bb_s)
        return lax.cond(kb % 2 == 0,
                        lambda t: att_step(kb, t, buf_a, buf_b),
                        lambda t: att_step(kb, t, buf_b, buf_a), ties_before)

    for g in range(kvg):
        logits_into(sa_s, 0, g)
    ties_before_1 = mask_into(ba_s, 0, jnp.zeros((1, tq), F32))
    lax.fori_loop(0, i + 1, att_blk, ties_before_1)

    for g in range(kvg):
        o_t = acc_s[g * hd:(g + 1) * hd, :] / l_s[g:g + 1, :]
        for hh in range(hpg):
            h = g * hpg + hh
            o_ref[:, h * hd:(h + 1) * hd] = o_t[:, hh * tq:(hh + 1) * tq].T.astype(o_ref.dtype)


def _dsa_attend(p_att, p_small, kn, vt, kin, q_norm_g, *, cfg, wi_blk, name):
    bsz, seq, _ = p_att.shape
    nh, hd, kvg = cfg.att_heads, cfg.att_head_dim, cfg.att_kv_groups
    ih, idim, tq = cfg.idx_heads, cfg.idx_dim, cfg.q_block
    assert nh * hd == ih * idim and ih <= LANE
    aw, kvw, nb, hpg = nh * hd, kvg * hd, seq // tq, nh // kvg
    ksel = min(cfg.topk_max, seq // 4)
    need = (2 * (2 * tq * aw * 4 + tq * LANE * 4 + 2 * seq * kvw * 2 + seq * idim * 2 + tq * aw * 2)
            + 2 * nh * tq * hd * 2 + seq * tq * 4 + 3 * nh * hd * tq * 4 + 16 * tq * tq * 4)
    return pl.pallas_call(
        functools.partial(_dsa_body, nh=nh, hd=hd, kvg=kvg, ih=ih, idim=idim, ksel=float(ksel),
                          eps=cfg.eps),
        grid=(bsz, nb),
        in_specs=[pl.BlockSpec((None, tq, aw), lambda b, i: (b, i, 0)),
                  pl.BlockSpec((None, tq, aw), lambda b, i: (b, i, 1)),
                  pl.BlockSpec((None, tq, LANE), lambda b, i: (b, i, wi_blk)),
                  pl.BlockSpec((None, seq, kvw), lambda b, i: (b, 0, 0)),
                  pl.BlockSpec((None, nb, kvw, tq), lambda b, i: (b, 0, 0, 0)),
                  pl.BlockSpec((None, seq, idim), lambda b, i: (b, 0, 0)),
                  pl.BlockSpec((1, hd), lambda b, i: (0, 0))],
        out_specs=pl.BlockSpec((None, tq, aw), lambda b, i: (b, i, 0)),
        out_shape=jax.ShapeDtypeStruct((bsz, seq, aw), BF16),
        scratch_shapes=[pltpu.VMEM((hd, nh * tq), BF16), pltpu.VMEM((idim, ih * tq), BF16),
                        pltpu.VMEM((nb + COUNT_UNROLL - 1, tq, tq), jnp.int32),
                        pltpu.VMEM((kvg, hpg * tq), F32), pltpu.VMEM((kvg, hpg * tq), F32),
                        pltpu.VMEM((kvg * hd, hpg * tq), F32),
                        pltpu.VMEM((kvg, tq, hpg * tq), F32), pltpu.VMEM((kvg, tq, hpg * tq), F32),
                        pltpu.VMEM((tq, tq), F32), pltpu.VMEM((tq, tq), F32)],
        compiler_params=_params(("parallel", "arbitrary"), need),
        name=name,
    )(p_att, p_att, p_small, kn, vt, kin, q_norm_g.reshape(1, hd))


def _pad_cols(w, width):
    return jnp.pad(w, ((0, 0), (0, width - w.shape[1])))


def _pack_w_in(w, cfg, d_model):
    d_inner = cfg.ssd_expand * d_model
    conv_ch = d_inner + 2 * cfg.ssd_groups * cfg.ssd_state
    heads = d_inner // cfg.ssd_head_dim
    aw = cfg.att_heads * cfg.att_head_dim
    kvw = cfg.att_kv_groups * cfg.att_head_dim
    sizes = (d_inner, conv_ch, heads, aw, kvw, kvw, cfg.idx_heads * cfg.idx_dim, cfg.idx_dim,
             cfg.idx_heads, d_model, d_model)
    assert sum(sizes) == w.shape[1]
    parts, o = [], 0
    for s in sizes:
        parts.append(w[:, o:o + s])
        o += s
    _, _, dt, q, k, v, qi, ki, wi, _, _ = parts
    hall = _round_up(heads, LANE)
    cat = lambda ws: jnp.concatenate(ws, axis=1).astype(BF16)
    return (w[:, :d_inner + conv_ch].astype(BF16), cat([q, qi, k, v]),
            w[:, w.shape[1] - 2 * d_model:].astype(BF16),
            cat([_pad_cols(dt, hall), ki, _pad_cols(wi, LANE)]), hall)


def _gated(acc, g):
    return _sigmoid(g) * acc


def _gated_add(acc, g, prev):
    return prev + _sigmoid(g) * acc


def _residual(acc, x, gate, gate_t):
    return x + (gate + gate_t) * acc


def _forward(cfg, x, c, w_ada, b_ada, ada_table, norm1_g, w_in, ssd_conv_w, ssd_conv_b,
             ssd_dt_bias, ssd_a_log, ssd_d, ssd_norm_g, w_ssd_out, q_norm_g, k_norm_g,
             idx_k_norm_g, w_att_out, w_o, norm2_g, w_up, ffn_conv_w, ffn_conv_b, w_down):
    bsz, seq, d = x.shape
    m = bsz * seq
    depth = w_in.shape[0]
    aw = cfg.att_heads * cfg.att_head_dim
    kvw = cfg.att_kv_groups * cfg.att_head_dim
    heads = cfg.ssd_expand * d // cfg.ssd_head_dim

    mod = _ada_mod(c, w_ada, b_ada).reshape(bsz, cfg.n_mod, d)
    shift1, scale1, gate1, shift2, scale2, gate2 = [mod[:, i:i + 1, :] for i in range(cfg.n_mod)]
    x2 = x.reshape(m, d)
    for l in range(depth):
        tab = ada_table[l]
        w_ssd, w_att, w_gate, w_small, hall = _pack_w_in(w_in[l], cfg, d)
        hpar = jnp.zeros((SUBLANE, hall), F32)
        hpar = hpar.at[0, :heads].set(ssd_dt_bias[l]).at[1, :heads].set(ssd_a_log[l])
        hpar = hpar.at[2, :heads].set(ssd_d[l])

        h = _norm_mod(x2, norm1_g[l], scale1, tab[1], shift1, tab[0], seq=seq, eps=cfg.eps,
                      name=f"norm1_l{l}")
        p_ssd = _matmul(h, w_ssd, out_dtype=F32, name=f"in_ssd_l{l}").reshape(bsz, seq, -1)
        p_att = _matmul(h, w_att, out_dtype=F32, name=f"in_att_l{l}").reshape(bsz, seq, -1)
        p_gate = _matmul(h, w_gate, out_dtype=F32, name=f"in_gate_l{l}")
        p_small = _matmul(h, w_small, out_dtype=F32, name=f"in_small_l{l}").reshape(bsz, seq, -1)

        y_ssd = _ssd_branch(p_ssd, p_small, hpar, ssd_conv_w[l], ssd_conv_b[l], ssd_norm_g[l],
                            cfg=cfg, name=f"ssd_l{l}")
        kn, vt, kin = _dsa_prep(p_att, p_small, k_norm_g[l], idx_k_norm_g[l], cfg=cfg,
                                k_blk=2 * aw // kvw, v_blk=2 * aw // kvw + 1,
                                ki_blk=hall // cfg.idx_dim, name=f"dsa_prep_l{l}")
        o_att = _dsa_attend(p_att, p_small, kn, vt, kin, q_norm_g[l], cfg=cfg,
                            wi_blk=(hall + cfg.idx_dim) // LANE, name=f"dsa_l{l}")

        part = _matmul(y_ssd.reshape(m, -1), w_ssd_out[l].astype(BF16), out_dtype=F32,
                       epilogue=_gated, tiles=[(p_gate, 0)], name=f"ssd_out_l{l}")
        merged = _matmul(o_att.reshape(m, -1), w_att_out[l].astype(BF16), out_dtype=BF16,
                         epilogue=_gated_add, tiles=[(p_gate, d), (part, 0)],
                         name=f"att_out_l{l}")
        x2 = _matmul(merged, w_o[l].astype(BF16), out_dtype=F32, epilogue=_residual,
                     tiles=[(x2, 0)], rows=[(gate1, seq), (tab[2].reshape(1, 1, d), m)],
                     name=f"mix_out_l{l}")

        h = _norm_mod(x2, norm2_g[l], scale2, tab[4], shift2, tab[3], seq=seq, eps=cfg.eps,
                      name=f"norm2_l{l}")
        act = _up_conv_gate(h, w_up[l].astype(BF16), ffn_conv_w[l], ffn_conv_b[l], seq=seq,
                            name=f"ffn_up_l{l}")
        x2 = _matmul(act, w_down[l].astype(BF16), out_dtype=F32, epilogue=_residual,
                     tiles=[(x2, 0)], rows=[(gate2, seq), (tab[5].reshape(1, 1, d), m)],
                     name=f"ffn_down_l{l}")
    return x2.reshape(bsz, seq, d)


def kernel(x, c, w_ada, b_ada, ada_table, norm1_g, w_in, ssd_conv_w, ssd_conv_b, ssd_dt_bias,
           ssd_a_log, ssd_d, ssd_norm_g, w_ssd_out, q_norm_g, k_norm_g, idx_k_norm_g, w_att_out,
           w_o, norm2_g, w_up, ffn_conv_w, ffn_conv_b, w_down):
    return _forward(Cfg(), x, c, w_ada, b_ada, ada_table, norm1_g, w_in, ssd_conv_w, ssd_conv_b,
                    ssd_dt_bias, ssd_a_log, ssd_d, ssd_norm_g, w_ssd_out, q_norm_g, k_norm_g,
                    idx_k_norm_g, w_att_out, w_o, norm2_g, w_up, ffn_conv_w, ffn_conv_b, w_down)
```
